```python
import jax, jax.numpy as jnp
from jax import lax
import numpy as np


D_MODEL = 2048
BATCH = 8
SEQ = 4096
DEPTH = 1
DEC_BATCH = 32
DEC_SEQ = 32
PAST_LEN = 4096

CHUNK = 64
M_HEADS = 8
M_WIDTH = 2048
M_HEAD_DIM = M_WIDTH // M_HEADS
CONV_WIDTH = 1024
CONV_K = 31
N_EXPERTS = 32
TOP_K = 4
D_EXPERT = 2048
SWIGLU_ALPHA = 1.702
SWIGLU_LIMIT = 7.0
PLE_DIM = 256
FORGET_BIAS = 3.0
RMS_EPS = 1e-6
LN_EPS = 1e-5
IN_SPLITS = (M_WIDTH, M_WIDTH, M_WIDTH, M_WIDTH, M_HEADS, M_HEADS, CONV_WIDTH, CONV_WIDTH, D_MODEL, D_MODEL)
IN_COLS = 4 * M_WIDTH + 2 * M_HEADS + 2 * CONV_WIDTH + 2 * D_MODEL
F_GATE_OFF = 4 * M_WIDTH + M_HEADS

kernel_name = 'streaming_mlstm_conformer_moe'


def _rmsnorm(x, g):
    xf = x.astype(jnp.float32)
    y = xf * lax.rsqrt(jnp.mean(xf * xf, axis=-1, keepdims=True) + RMS_EPS)
    return (y * g.astype(jnp.float32)).astype(x.dtype)


def _layernorm(x, g, b):
    xf = x.astype(jnp.float32)
    mu = jnp.mean(xf, axis=-1, keepdims=True)
    xc = xf - mu
    y = xc * lax.rsqrt(jnp.mean(xc * xc, axis=-1, keepdims=True) + LN_EPS)
    return (y * g.astype(jnp.float32) + b.astype(jnp.float32)).astype(x.dtype)


def _split_cols(proj):
    out = []
    start = 0
    for size in IN_SPLITS:
        out.append(proj[..., start:start + size])
        start += size
    return out


def _mlstm_chunk(carry, inp):
    C0, n0, m0 = carry
    q, k, v, ig, lf = inp
    L = q.shape[2]
    F = jnp.cumsum(lf, axis=-1)
    causal = jnp.tril(jnp.ones((L, L), dtype=bool))
    D = jnp.where(causal, F[..., :, None] - F[..., None, :] + ig[..., None, :], -jnp.inf)
    inter = F + m0[..., None]
    m = jnp.maximum(inter, jnp.max(D, axis=-1))
    a = jnp.exp(inter - m)
    W = jnp.exp(D - m[..., None]) * jnp.einsum('bhtd,bhsd->bhts', q, k)
    num = a[..., None] * jnp.einsum('bhvd,bhtd->bhtv', C0, q) + jnp.einsum('bhts,bhsv->bhtv', W, v)
    den_raw = a * jnp.einsum('bhd,bhtd->bht', n0, q) + jnp.sum(W, axis=-1)
    den = jnp.maximum(jnp.abs(den_raw), jnp.exp(-m))
    h = num / den[..., None]
    mL = m[..., -1]
    w_end = jnp.exp(F[..., -1:] - F + ig - mL[..., None])
    decay = jnp.exp(inter[..., -1] - mL)
    C1 = decay[..., None, None] * C0 + jnp.einsum('bhs,bhsv,bhsd->bhvd', w_end, v, k)
    n1 = decay[..., None] * n0 + jnp.einsum('bhs,bhsd->bhd', w_end, k)
    return (C1, n1, mL), h


def _mlstm(q, k, v, ig, lf, state):
    B, T = q.shape[0], q.shape[1]
    L = CHUNK if T % CHUNK == 0 else T
    nC = T // L

    def blk(a):
        return a.reshape(B, nC, L, M_HEADS, M_HEAD_DIM).transpose(1, 0, 3, 2, 4)

    def blk_g(a):
        return a.reshape(B, nC, L, M_HEADS).transpose(1, 0, 3, 2)

    new_state, h = lax.scan(_mlstm_chunk, state, (blk(q), blk(k), blk(v), blk_g(ig), blk_g(lf)))
    h = h.transpose(1, 0, 3, 2, 4).reshape(B, T, M_WIDTH)
    return h, new_state


def _conv_module(u, buf, dw_w, dw_b, ln_g, ln_b, pw_w, pw_b):
    full = jnp.concatenate([buf.astype(u.dtype), u], axis=1)
    y = lax.conv_general_dilated(full, dw_w[:, None, :].astype(u.dtype), window_strides=(1,), padding='VALID',
                                 dimension_numbers=('NWC', 'WIO', 'NWC'), feature_group_count=CONV_WIDTH) + dw_b
    y = jax.nn.silu(_layernorm(y, ln_g, ln_b))
    return y @ pw_w + pw_b, full[:, -(CONV_K - 1):]


def _moe(x, r_w, r_b, wg, bg, wu, bu, wd, bd):
    B, T, D = x.shape
    xt = x.reshape(B * T, D)
    logits = (xt @ r_w + r_b).astype(jnp.float32)
    top_v, top_i = lax.top_k(logits, TOP_K)
    top_w = jax.nn.softmax(top_v, axis=-1)
    gates = jnp.sum(jax.nn.one_hot(top_i, N_EXPERTS, dtype=jnp.float32) * top_w[..., None], axis=1)
    y = jnp.zeros((B * T, D), jnp.float32)
    for e in range(N_EXPERTS):
        g = jnp.minimum(xt @ wg[e] + bg[e], SWIGLU_LIMIT)
        u = jnp.clip(xt @ wu[e] + bu[e], -SWIGLU_LIMIT, SWIGLU_LIMIT)
        hid = g * jax.nn.sigmoid(SWIGLU_ALPHA * g) * (u + 1)
        y = y + gates[:, e:e + 1] * (hid @ wd[e] + bd[e]).astype(jnp.float32)
    return y.astype(x.dtype).reshape(B, T, D)


def _layer(x, p, state, params):
    (g_mix, w_in, b_in, w_a, dw_w, dw_b, ln_g, ln_b, pw_w, pw_b, w_o, g_ffn,
     r_w, r_b, wg, bg, wu, bu, wd, bd, g_ple, w_ple, w_pg) = params
    C0, n0, m0, buf = state
    B, T, _ = x.shape
    f32 = jnp.float32
    xn = _rmsnorm(x, g_mix)
    q, k, v, o, ig, fg, glu_a, glu_b, gate_a, gate_b = _split_cols(xn @ w_in + b_in)
    qh = q.astype(f32).reshape(B, T, M_HEADS, M_HEAD_DIM)
    kh = k.astype(f32).reshape(B, T, M_HEADS, M_HEAD_DIM) * (M_HEAD_DIM ** -0.5)
    vh = v.astype(f32).reshape(B, T, M_HEADS, M_HEAD_DIM)
    h, (C1, n1, m1) = _mlstm(qh, kh, vh, ig.astype(f32), jax.nn.log_sigmoid(fg.astype(f32)), (C0, n0, m0))
    a_out = (jax.nn.sigmoid(o) * h.astype(x.dtype)) @ w_a
    b_out, buf1 = _conv_module(glu_a * jax.nn.sigmoid(glu_b), buf, dw_w, dw_b, ln_g, ln_b, pw_w, pw_b)
    x = x + (jax.nn.sigmoid(gate_a) * a_out + jax.nn.sigmoid(gate_b) * b_out) @ w_o
    x = x + _moe(_rmsnorm(x, g_ffn), r_w, r_b, wg, bg, wu, bu, wd, bd)
    x = x + (p.astype(x.dtype) @ w_ple) * jax.nn.sigmoid(_rmsnorm(x, g_ple) @ w_pg)
    return x, (C1, n1, m1, buf1)


def setup_inputs(seed: int = 0) -> dict:
    key = jax.random.key(seed)
    ks = jax.random.split(key, 40)

    def nrm(k, shape, scale):
        return jax.random.normal(k, shape, jnp.float32) * scale

    return {
        'x_prompt': nrm(ks[0], (BATCH, SEQ, D_MODEL), 1.0),
        'x_sample': nrm(ks[1], (DEC_BATCH, DEC_SEQ, D_MODEL), 1.0),
        'state_mlstm_C': nrm(ks[2], (DEPTH, DEC_BATCH, M_HEADS, M_HEAD_DIM, M_HEAD_DIM), 0.05),
        'state_mlstm_n': nrm(ks[3], (DEPTH, DEC_BATCH, M_HEADS, M_HEAD_DIM), 0.5),
        'state_mlstm_m': nrm(ks[4], (DEPTH, DEC_BATCH, M_HEADS), 1.0),
        'cache_conv': nrm(ks[5], (DEPTH, DEC_BATCH, CONV_K - 1, CONV_WIDTH), 0.5),
        'p_prompt': nrm(ks[6], (DEPTH, BATCH, SEQ, PLE_DIM), 1.0),
        'p_sample': nrm(ks[7], (DEPTH, DEC_BATCH, DEC_SEQ, PLE_DIM), 1.0),
        'norm_mix': 1.0 + nrm(ks[8], (DEPTH, D_MODEL), 0.02),
        'w_in': nrm(ks[9], (DEPTH, D_MODEL, IN_COLS), D_MODEL ** -0.5),
        'b_in': nrm(ks[10], (DEPTH, IN_COLS), 0.02).at[:, F_GATE_OFF:F_GATE_OFF + M_HEADS].add(FORGET_BIAS),
        'w_a': nrm(ks[11], (DEPTH, M_WIDTH, D_MODEL), M_WIDTH ** -0.5),
        'conv_dw_w': nrm(ks[12], (DEPTH, CONV_K, CONV_WIDTH), CONV_K ** -0.5),
        'conv_dw_b': nrm(ks[13], (DEPTH, CONV_WIDTH), 0.02),
        'conv_ln_g': 1.0 + nrm(ks[14], (DEPTH, CONV_WIDTH), 0.02),
        'conv_ln_b': nrm(ks[15], (DEPTH, CONV_WIDTH), 0.02),
        'conv_pw_w': nrm(ks[16], (DEPTH, CONV_WIDTH, D_MODEL), CONV_WIDTH ** -0.5),
        'conv_pw_b': nrm(ks[17], (DEPTH, D_MODEL), 0.02),
        'w_o': nrm(ks[18], (DEPTH, D_MODEL, D_MODEL), D_MODEL ** -0.5),
        'norm_ffn': 1.0 + nrm(ks[19], (DEPTH, D_MODEL), 0.02),
        'router_w': nrm(ks[20], (DEPTH, D_MODEL, N_EXPERTS), D_MODEL ** -0.5),
        'router_b': nrm(ks[21], (DEPTH, N_EXPERTS), 0.01),
        'exp_w_gate': nrm(ks[22], (DEPTH, N_EXPERTS, D_MODEL, D_EXPERT), D_MODEL ** -0.5),
        'exp_b_gate': nrm(ks[23], (DEPTH, N_EXPERTS, D_EXPERT), 0.02),
        'exp_w_up': nrm(ks[24], (DEPTH, N_EXPERTS, D_MODEL, D_EXPERT), D_MODEL ** -0.5),
        'exp_b_up': nrm(ks[25], (DEPTH, N_EXPERTS, D_EXPERT), 0.02),
        'exp_w_down': nrm(ks[26], (DEPTH, N_EXPERTS, D_EXPERT, D_MODEL), D_EXPERT ** -0.5),
        'exp_b_down': nrm(ks[27], (DEPTH, N_EXPERTS, D_MODEL), 0.02),
        'norm_ple': 1.0 + nrm(ks[28], (DEPTH, D_MODEL), 0.02),
        'w_ple': nrm(ks[29], (DEPTH, PLE_DIM, D_MODEL), PLE_DIM ** -0.5),
        'w_ple_gate': nrm(ks[30], (DEPTH, D_MODEL, D_MODEL), D_MODEL ** -0.5),
        'norm_final': 1.0 + nrm(ks[31], (D_MODEL,), 0.02),
    }


def reference(x_prompt, x_sample, state_mlstm_C, state_mlstm_n, state_mlstm_m, cache_conv, p_prompt, p_sample,
              norm_mix, w_in, b_in, w_a, conv_dw_w, conv_dw_b, conv_ln_g, conv_ln_b, conv_pw_w, conv_pw_b, w_o,
              norm_ffn, router_w, router_b, exp_w_gate, exp_b_gate, exp_w_up, exp_b_up, exp_w_down, exp_b_down,
              norm_ple, w_ple, w_ple_gate, norm_final):
    f32 = jnp.float32
    bp = x_prompt.shape[0]
    xp, xs = x_prompt, x_sample
    Cp, n_p, m_p, cp = [], [], [], []
    Cs, n_s, m_s, cs = [], [], [], []
    for l in range(DEPTH):
        params = (norm_mix[l], w_in[l], b_in[l], w_a[l], conv_dw_w[l], conv_dw_b[l], conv_ln_g[l], conv_ln_b[l],
                  conv_pw_w[l], conv_pw_b[l], w_o[l], norm_ffn[l], router_w[l], router_b[l],
                  exp_w_gate[l], exp_b_gate[l], exp_w_up[l], exp_b_up[l], exp_w_down[l], exp_b_down[l],
                  norm_ple[l], w_ple[l], w_ple_gate[l])
        zero_state = (jnp.zeros((bp, M_HEADS, M_HEAD_DIM, M_HEAD_DIM), f32),
                      jnp.zeros((bp, M_HEADS, M_HEAD_DIM), f32),
                      jnp.zeros((bp, M_HEADS), f32),
                      jnp.zeros((bp, CONV_K - 1, CONV_WIDTH), xp.dtype))
        xp, (c1, n1, m1, b1) = _layer(xp, p_prompt[l], zero_state, params)
        Cp.append(c1); n_p.append(n1); m_p.append(m1); cp.append(b1)
        carried = (state_mlstm_C[l].astype(f32), state_mlstm_n[l].astype(f32),
                   state_mlstm_m[l].astype(f32), cache_conv[l])
        xs, (c2, n2, m2, b2) = _layer(xs, p_sample[l], carried, params)
        Cs.append(c2); n_s.append(n2); m_s.append(m2); cs.append(b2)
    y_prompt = _rmsnorm(xp, norm_final)
    y_sample = _rmsnorm(xs, norm_final)
    return (y_prompt, y_sample, jnp.stack(Cp), jnp.stack(n_p), jnp.stack(m_p), jnp.stack(cp),
            jnp.stack(Cs), jnp.stack(n_s), jnp.stack(m_s), jnp.stack(cs))
```

```python
import functools

import jax
import jax.numpy as jnp
from jax import lax
from jax.experimental import pallas as pl
from jax.experimental.pallas import tpu as pltpu

F32 = jnp.float32
BF16 = jnp.bfloat16
I32 = jnp.int32

RMS_EPS = 1e-6
LN_EPS = 1e-5
TOP_K = 4
SWIGLU_ALPHA = 1.702
SWIGLU_LIMIT = 7.0
CHUNK = 64
LANES = 128
HALO = 32
VMEM_LIMIT = 56 * 1024 * 1024


def _cparams(*sem):
    return pltpu.CompilerParams(dimension_semantics=sem, vmem_limit_bytes=VMEM_LIMIT)


def _rms(x, g):
    ms = jnp.mean(x * x, axis=-1, keepdims=True)
    return x * lax.rsqrt(ms + RMS_EPS) * g


def _split_bf16(x):
    hi = x.astype(BF16)
    lo = (x - hi.astype(F32)).astype(BF16)
    return hi, lo


def _norm_mm_kernel(x_ref, g_ref, w_ref, b_ref, o_ref, xn_ref, *, act):
    @pl.when(pl.program_id(1) == 0)
    def _():
        xn_ref[...] = _rms(x_ref[...], g_ref[...]).astype(BF16)

    acc = jnp.dot(xn_ref[...], w_ref[...], preferred_element_type=F32) + b_ref[...]
    if act == "sigmoid":
        acc = jax.nn.sigmoid(acc)
    o_ref[...] = acc.astype(o_ref.dtype)


def _norm_mm(x, g, w, b, *, act, out_dtype, tm, tn):
    n, d = x.shape
    cols = w.shape[1]
    return pl.pallas_call(
        functools.partial(_norm_mm_kernel, act=act),
        grid=(n // tm, cols // tn),
        in_specs=[pl.BlockSpec((tm, d), lambda i, j: (i, 0)),
                  pl.BlockSpec((1, d), lambda i, j: (0, 0)),
                  pl.BlockSpec((d, tn), lambda i, j: (0, j)),
                  pl.BlockSpec((1, tn), lambda i, j: (0, j))],
        out_specs=pl.BlockSpec((tm, tn), lambda i, j: (i, j)),
        out_shape=jax.ShapeDtypeStruct((n, cols), out_dtype),
        scratch_shapes=[pltpu.VMEM((tm, d), BF16)],
        compiler_params=_cparams("parallel", "arbitrary"),
        name="norm_mm_" + str(act),
    )(x, g, w, b)


def _gates_glu_kernel(x_ref, g_ref, wgh_ref, wgl_ref, bg_ref, wa_ref, ba_ref, wb_ref, bb_ref,
                      gate_ref, u_ref, xh_ref, xl_ref):
    @pl.when(pl.program_id(1) == 0)
    def _():
        xn = _rms(x_ref[...], g_ref[...])
        hi, lo = _split_bf16(xn)
        xh_ref[...] = hi
        xl_ref[...] = lo
        gate_ref[...] = (jnp.dot(hi, wgh_ref[...], preferred_element_type=F32)
                         + jnp.dot(hi, wgl_ref[...], preferred_element_type=F32)
                         + jnp.dot(lo, wgh_ref[...], preferred_element_type=F32)
                         + bg_ref[...])

    xh = xh_ref[...]
    a = jnp.dot(xh, wa_ref[...], preferred_element_type=F32) + ba_ref[...]
    b = jnp.dot(xh, wb_ref[...], preferred_element_type=F32) + bb_ref[...]
    u_ref[...] = a * jax.nn.sigmoid(b)


def _gates_glu(x, g, wgh, wgl, bg, wa, ba, wb, bb, *, tm, tn):
    n, d = x.shape
    cw = wa.shape[1]
    return pl.pallas_call(
        _gates_glu_kernel,
        grid=(n // tm, cw // tn),
        in_specs=[pl.BlockSpec((tm, d), lambda i, j: (i, 0)),
                  pl.BlockSpec((1, d), lambda i, j: (0, 0)),
                  pl.BlockSpec((d, LANES), lambda i, j: (0, 0)),
                  pl.BlockSpec((d, LANES), lambda i, j: (0, 0)),
                  pl.BlockSpec((1, LANES), lambda i, j: (0, 0)),
                  pl.BlockSpec((d, tn), lambda i, j: (0, j)),
                  pl.BlockSpec((1, tn), lambda i, j: (0, j)),
                  pl.BlockSpec((d, tn), lambda i, j: (0, j)),
                  pl.BlockSpec((1, tn), lambda i, j: (0, j))],
        out_specs=[pl.BlockSpec((tm, LANES), lambda i, j: (i, 0)),
                   pl.BlockSpec((tm, tn), lambda i, j: (i, j))],
        out_shape=[jax.ShapeDtypeStruct((n, LANES), F32),
                   jax.ShapeDtypeStruct((n, cw), F32)],
        scratch_shapes=[pltpu.VMEM((tm, d), BF16), pltpu.VMEM((tm, d), BF16)],
        compiler_params=_cparams("parallel", "arbitrary"),
        name="gates_glu",
    )(x, g, wgh, wgl, bg, wa, ba, wb, bb)


def _mlstm_kernel(q_ref, k_ref, v_ref, o_ref, gc_ref, gr_ref, c0_ref, n0_ref, m0_ref,
                  hs_ref, c1_ref, n1_ref, m1_ref, c_sc, n_sc, m_sc, *, heads, dh, scale):
    c = pl.program_id(1)
    nc = pl.num_programs(1)
    L = q_ref.shape[0]

    @pl.when(c == 0)
    def _():
        c_sc[...] = c0_ref[0]
        n_sc[...] = n0_ref[0]
        m_sc[...] = m0_ref[0]

    row = lax.broadcasted_iota(I32, (L, L), 0)
    col = lax.broadcasted_iota(I32, (L, L), 1)
    causal = col <= row
    gc = gc_ref[...]
    gr = gr_ref[0]

    for h in range(heads):
        sl = slice(h * dh, (h + 1) * dh)
        q = q_ref[:, sl]
        ks = k_ref[:, sl] * scale
        v = v_ref[:, sl]
        ig_col = gc[:, h:h + 1]
        lf_col = jax.nn.log_sigmoid(gc[:, heads + h:heads + h + 1])
        ig_row = gr[h:h + 1, :]
        lf_row = jax.nn.log_sigmoid(gr[heads + h:heads + h + 1, :])
        f_col = jnp.sum(jnp.where(causal, lf_row, 0.0), axis=1, keepdims=True)
        f_row = jnp.sum(jnp.where(row <= col, lf_col, 0.0), axis=0, keepdims=True)
        m0 = m_sc[h]
        n0 = n_sc[h]
        c0 = c_sc[h]
        dmat = jnp.where(causal, f_col - f_row + ig_row, -jnp.inf)
        inter = f_col + m0
        m_col = jnp.maximum(inter, jnp.max(dmat, axis=1, keepdims=True))
        a_col = jnp.exp(inter - m_col)
        s = lax.dot_general(q, ks, (((1,), (1,)), ((), ())), preferred_element_type=F32)
        w = jnp.exp(dmat - m_col) * s
        qc = lax.dot_general(q, c0.astype(BF16), (((1,), (1,)), ((), ())), preferred_element_type=F32)
        num = a_col * qc + jnp.dot(w.astype(BF16), v, preferred_element_type=F32)
        qf = q.astype(F32)
        den_raw = a_col * jnp.sum(qf * n0, axis=1, keepdims=True) + jnp.sum(w, axis=1, keepdims=True)
        den = jnp.maximum(jnp.abs(den_raw), jnp.exp(-m_col))
        hval = num / den
        hs_ref[:, sl] = (jax.nn.sigmoid(o_ref[:, sl].astype(F32)) * hval).astype(hs_ref.dtype)
        m_last = m_col[L - 1:L, :]
        f_last = f_col[L - 1:L, :]
        w_end = jnp.exp(f_last - f_col + ig_col - m_last)
        decay = jnp.exp(inter[L - 1:L, :] - m_last)
        vw = (v.astype(F32) * w_end).astype(BF16)
        c_sc[h] = decay * c0 + lax.dot_general(vw, ks, (((0,), (0,)), ((), ())), preferred_element_type=F32)
        n_sc[h] = decay * n0 + jnp.sum(ks.astype(F32) * w_end, axis=0, keepdims=True)
        m_sc[h] = m_last

    @pl.when(c == nc - 1)
    def _():
        c1_ref[0] = c_sc[...]
        n1_ref[0] = n_sc[...]
        m1_ref[0] = m_sc[...]


def _mlstm(qkvo, gates_c, gates_r, c0, n0, m0, *, row0, batch, seq, chunk, heads, dh):
    width = heads * dh
    nc = seq // chunk
    rb0 = row0 // chunk
    n0r = n0.reshape(batch, heads, 1, dh)
    m0r = m0.reshape(batch, heads, 1, 1)

    def blk(part):
        return pl.BlockSpec((chunk, width), lambda b, c: (rb0 + b * nc + c, part))

    state_spec = lambda shp: pl.BlockSpec((1,) + shp, lambda b, c: (b, 0, 0, 0))
    hs, c1, n1, m1 = pl.pallas_call(
        functools.partial(_mlstm_kernel, heads=heads, dh=dh, scale=dh ** -0.5),
        grid=(batch, nc),
        in_specs=[blk(0), blk(1), blk(2), blk(3),
                  pl.BlockSpec((chunk, LANES), lambda b, c: (rb0 + b * nc + c, 0)),
                  pl.BlockSpec((1, 2 * heads, chunk), lambda b, c: (b * nc + c, 0, 0)),
                  state_spec((heads, dh, dh)), state_spec((heads, 1, dh)), state_spec((heads, 1, 1))],
        out_specs=[pl.BlockSpec((chunk, width), lambda b, c: (b * nc + c, 0)),
                   state_spec((heads, dh, dh)), state_spec((heads, 1, dh)), state_spec((heads, 1, 1))],
        out_shape=[jax.ShapeDtypeStruct((batch * seq, width), BF16),
                   jax.ShapeDtypeStruct((batch, heads, dh, dh), F32),
                   jax.ShapeDtypeStruct((batch, heads, 1, dh), F32),
                   jax.ShapeDtypeStruct((batch, heads, 1, 1), F32)],
        scratch_shapes=[pltpu.VMEM((heads, dh, dh), F32), pltpu.VMEM((heads, 1, dh), F32),
                        pltpu.VMEM((heads, 1, 1), F32)],
        compiler_params=_cparams("parallel", "arbitrary"),
        name="mlstm",
    )(qkvo, qkvo, qkvo, qkvo, gates_c, gates_r, c0, n0r, m0r)
    return hs, c1, n1.reshape(batch, heads, dh), m1.reshape(batch, heads)


def _conv_kernel(halo_ref, u_ref, dw_ref, dwb_ref, lg_ref, lb_ref, pw_ref, pwb_ref, o_ref, full_sc, y_sc,
                 *, taps, zero_first, rb, cb):
    tt, cw = u_ref.shape
    halo = halo_ref[...]
    if zero_first:
        halo = jnp.where(pl.program_id(1) == 0, 0.0, halo)
    full_sc[0:HALO, :] = halo
    full_sc[HALO:HALO + tt, :] = u_ref[...]
    base = HALO - (taps - 1)

    def body(r, carry):
        r0 = pl.multiple_of(r * rb, rb)
        for c0 in range(0, cw, cb):
            win = full_sc[pl.ds(r0, rb + HALO), c0:c0 + cb]
            wts = dw_ref[:, c0:c0 + cb]
            acc = jnp.zeros((rb, cb), F32) + dwb_ref[:, c0:c0 + cb]
            for j in range(taps):
                acc = acc + wts[j:j + 1, :] * win[base + j:base + j + rb, :]
            y_sc[pl.ds(r0, rb), c0:c0 + cb] = acc
        return carry

    lax.fori_loop(0, tt // rb, body, 0)
    y = y_sc[...]
    mu = jnp.mean(y, axis=-1, keepdims=True)
    yc = y - mu
    yn = yc * lax.rsqrt(jnp.mean(yc * yc, axis=-1, keepdims=True) + LN_EPS) * lg_ref[...] + lb_ref[...]
    act = yn * jax.nn.sigmoid(yn)
    o_ref[...] = (jnp.dot(act.astype(BF16), pw_ref[...], preferred_element_type=F32) + pwb_ref[...]).astype(o_ref.dtype)


def _conv(u, halo_src, dw_w, dw_b, ln_g, ln_b, pw_w, pw_b, *, row0, batch, seq, tt, zero_first):
    cw = u.shape[1]
    d = pw_w.shape[1]
    taps = dw_w.shape[0]
    nt = seq // tt
    tb0 = row0 // tt
    if zero_first:
        hb0, per = row0 // HALO, tt // HALO
        halo_map = lambda b, t: (jnp.maximum(hb0 + (b * nt + t) * per - 1, 0), 0)
    else:
        assert nt == 1, "a carried conv cache is only supported for single-tile sequences"
        halo_map = lambda b, t: (b, 0)
    dw_pad = jnp.zeros((HALO, cw), F32).at[:taps].set(dw_w)
    return pl.pallas_call(
        functools.partial(_conv_kernel, taps=taps, zero_first=zero_first, rb=16, cb=min(256, cw)),
        grid=(batch, nt),
        in_specs=[pl.BlockSpec((HALO, cw), halo_map),
                  pl.BlockSpec((tt, cw), lambda b, t: (tb0 + b * nt + t, 0)),
                  pl.BlockSpec((HALO, cw), lambda b, t: (0, 0)),
                  pl.BlockSpec((1, cw), lambda b, t: (0, 0)),
                  pl.BlockSpec((1, cw), lambda b, t: (0, 0)),
                  pl.BlockSpec((1, cw), lambda b, t: (0, 0)),
                  pl.BlockSpec((cw, d), lambda b, t: (0, 0)),
                  pl.BlockSpec((1, d), lambda b, t: (0, 0))],
        out_specs=pl.BlockSpec((tt, d), lambda b, t: (b * nt + t, 0)),
        out_shape=jax.ShapeDtypeStruct((batch * seq, d), BF16),
        scratch_shapes=[pltpu.VMEM((HALO + tt, cw), F32), pltpu.VMEM((tt, cw), F32)],
        compiler_params=_cparams("parallel", "arbitrary"),
        name="conv_module",
    )(halo_src, u, dw_pad, dw_b, ln_g, ln_b, pw_w, pw_b)


def _merge_kernel(x_ref, hs_ref, bo_ref, ga_ref, gb_ref, wa_ref, wo_ref, g_ref, rwh_ref, rwl_ref, rb_ref,
                  x1_ref, xn_ref, lg_ref, *, n_exp):
    a = jnp.dot(hs_ref[...], wa_ref[...], preferred_element_type=F32)
    mix = ga_ref[...].astype(F32) * a + gb_ref[...].astype(F32) * bo_ref[...].astype(F32)
    x1 = x_ref[...] + jnp.dot(mix.astype(BF16), wo_ref[...], preferred_element_type=F32)
    x1_ref[...] = x1
    xn = _rms(x1, g_ref[...])
    xn_ref[...] = xn
    hi, lo = _split_bf16(xn)
    logits = (jnp.dot(hi, rwh_ref[...], preferred_element_type=F32)
              + jnp.dot(hi, rwl_ref[...], preferred_element_type=F32)
              + jnp.dot(lo, rwh_ref[...], preferred_element_type=F32)
              + rb_ref[...])
    lane = lax.broadcasted_iota(I32, logits.shape, 1)
    lg_ref[...] = jnp.where(lane < n_exp, logits, -jnp.inf)


def _merge(x, hs, bo, gab, w_a, w_o, g, rwh, rwl, rb, *, n_exp, tm):
    n, d = x.shape
    mw = hs.shape[1]
    const = lambda shp: pl.BlockSpec(shp, lambda i: (0, 0), pipeline_mode=pl.Buffered(1))
    return pl.pallas_call(
        functools.partial(_merge_kernel, n_exp=n_exp),
        grid=(n // tm,),
        in_specs=[pl.BlockSpec((tm, d), lambda i: (i, 0)),
                  pl.BlockSpec((tm, mw), lambda i: (i, 0)),
                  pl.BlockSpec((tm, d), lambda i: (i, 0)),
                  pl.BlockSpec((tm, d), lambda i: (i, 0)),
                  pl.BlockSpec((tm, d), lambda i: (i, 1)),
                  const((mw, d)), const((d, d)), const((1, d)),
                  const((d, LANES)), const((d, LANES)), const((1, LANES))],
        out_specs=[pl.BlockSpec((tm, d), lambda i: (i, 0)),
                   pl.BlockSpec((tm, d), lambda i: (i, 0)),
                   pl.BlockSpec((tm, LANES), lambda i: (i, 0))],
        out_shape=[jax.ShapeDtypeStruct((n, d), F32),
                   jax.ShapeDtypeStruct((n, d), F32),
                   jax.ShapeDtypeStruct((n, LANES), F32)],
        compiler_params=_cparams("parallel"),
        name="merge",
    )(x, hs, bo, gab, gab, w_a, w_o, g, rwh, rwl, rb)


def _router_kernel(lg_ref, w_ref, e_ref, r_ref, cnt_ref, carry_sc):
    i = pl.program_id(0)

    @pl.when(i == 0)
    def _():
        carry_sc[...] = jnp.zeros_like(carry_sc)

    l = lg_ref[...]
    tm = l.shape[0]
    lane = lax.broadcasted_iota(I32, l.shape, 1)
    lane_f = lane.astype(F32)
    vals, hots, idxs = [], [], []
    for _ in range(TOP_K):
        mx = jnp.max(l, axis=1, keepdims=True)
        idx = jnp.min(jnp.where(l == mx, lane_f, float(LANES)), axis=1, keepdims=True)
        hot = lane_f == idx
        vals.append(mx)
        hots.append(hot)
        idxs.append(idx)
        l = jnp.where(hot, -jnp.inf, l)
    ex = [jnp.exp(v - vals[0]) for v in vals]
    tot = ex[0] + ex[1] + ex[2] + ex[3]
    sel = jnp.zeros(l.shape, F32)
    for hot in hots:
        sel = jnp.where(hot, 1.0, sel)
    rr = lax.broadcasted_iota(I32, (tm, tm), 0)
    cc = lax.broadcasted_iota(I32, (tm, tm), 1)
    tri = jnp.where(cc < rr, 1.0, 0.0).astype(BF16)
    cum = jnp.dot(tri, sel.astype(BF16), preferred_element_type=F32) + carry_sc[...]
    w_out = jnp.zeros(l.shape, F32)
    e_out = jnp.zeros(l.shape, I32)
    r_out = jnp.zeros(l.shape, I32)
    for k in range(TOP_K):
        rank = jnp.sum(jnp.where(hots[k], cum, 0.0), axis=1, keepdims=True).astype(I32)
        w_out = jnp.where(lane == k, ex[k] / tot, w_out)
        e_out = jnp.where(lane == k, idxs[k].astype(I32), e_out)
        r_out = jnp.where(lane == k, rank, r_out)
    w_ref[...] = w_out
    e_ref[...] = e_out
    r_ref[...] = r_out
    carry_sc[...] = carry_sc[...] + jnp.sum(sel, axis=0, keepdims=True)
    cnt_ref[...] = carry_sc[...]


def _router(logits, *, tm):
    n = logits.shape[0]
    blk = pl.BlockSpec((tm, LANES), lambda i: (i, 0))
    return pl.pallas_call(
        _router_kernel,
        grid=(n // tm,),
        in_specs=[blk],
        out_specs=[blk, blk, blk, pl.BlockSpec((1, LANES), lambda i: (0, 0))],
        out_shape=[jax.ShapeDtypeStruct((n, LANES), F32), jax.ShapeDtypeStruct((n, LANES), I32),
                   jax.ShapeDtypeStruct((n, LANES), I32), jax.ShapeDtypeStruct((1, LANES), F32)],
        scratch_shapes=[pltpu.VMEM((1, LANES), F32)],
        compiler_params=_cparams("arbitrary"),
        name="router",
    )(logits)


def _dispatch_kernel(slot_ref, x_ref, xs_in_ref, xs_ref, sem):
    del xs_in_ref
    tm = x_ref.shape[0]

    def row_copy(r, k):
        return pltpu.make_async_copy(x_ref.at[pl.ds(r, 1)], xs_ref.at[pl.ds(slot_ref[0, 0, r * TOP_K + k], 1)], sem)

    def start(r, carry):
        for k in range(TOP_K):
            row_copy(r, k).start()
        return carry

    def wait(r, carry):
        for k in range(TOP_K):
            row_copy(r, k).wait()
        return carry

    lax.fori_loop(0, tm, start, 0)
    lax.fori_loop(0, tm, wait, 0)


def _dispatch(xn, slots, n_slots, *, tm):
    n, d = xn.shape
    xs0 = jnp.zeros((n_slots, d), xn.dtype)
    return pl.pallas_call(
        _dispatch_kernel,
        grid=(n // tm,),
        in_specs=[pl.BlockSpec((1, 1, tm * TOP_K), lambda i: (i, 0, 0), memory_space=pltpu.SMEM),
                  pl.BlockSpec((tm, d), lambda i: (i, 0)),
                  pl.BlockSpec(memory_space=pl.ANY)],
        out_specs=pl.BlockSpec(memory_space=pl.ANY),
        out_shape=jax.ShapeDtypeStruct((n_slots, d), xn.dtype),
        scratch_shapes=[pltpu.SemaphoreType.DMA(())],
        input_output_aliases={2: 0},
        compiler_params=_cparams("arbitrary"),
        name="dispatch",
    )(slots.reshape(n // tm, 1, tm * TOP_K), xn, xs0)


def _tile_is_used(te_ref):
    return pl.program_id(1) < te_ref[pl.num_programs(1)]


def _up_kernel(te_ref, xs_ref, wg_ref, bg_ref, wu_ref, bu_ref, h_ref):
    @pl.when(_tile_is_used(te_ref))
    def _():
        x = xs_ref[...].astype(BF16)
        g = jnp.minimum(jnp.dot(x, wg_ref[0], preferred_element_type=F32) + bg_ref[0], SWIGLU_LIMIT)
        u = jnp.clip(jnp.dot(x, wu_ref[0], preferred_element_type=F32) + bu_ref[0], -SWIGLU_LIMIT, SWIGLU_LIMIT)
        h_ref[...] = (g * jax.nn.sigmoid(SWIGLU_ALPHA * g) * (u + 1.0)).astype(h_ref.dtype)

    @pl.when(jnp.logical_not(_tile_is_used(te_ref)))
    def _():
        h_ref[...] = jnp.zeros_like(h_ref)


def _down_kernel(te_ref, h_ref, wd_ref, bd_ref, o_ref):
    @pl.when(_tile_is_used(te_ref))
    def _():
        o_ref[...] = (jnp.dot(h_ref[...], wd_ref[0], preferred_element_type=F32) + bd_ref[0]).astype(o_ref.dtype)

    @pl.when(jnp.logical_not(_tile_is_used(te_ref)))
    def _():
        o_ref[...] = jnp.zeros_like(o_ref)


def _expert_up(tile_exp, xs, wg, bg, wu, bu, *, tm, tn):
    s, d = xs.shape
    de = wg.shape[2]
    wspec = pl.BlockSpec((1, d, tn), lambda j, i, te: (te[i], 0, j))
    bspec = pl.BlockSpec((1, 1, tn), lambda j, i, te: (te[i], 0, j))
    return pl.pallas_call(
        _up_kernel,
        grid_spec=pltpu.PrefetchScalarGridSpec(
            num_scalar_prefetch=1,
            grid=(de // tn, s // tm),
            in_specs=[pl.BlockSpec((tm, d), lambda j, i, te: (i, 0)), wspec, bspec, wspec, bspec],
            out_specs=pl.BlockSpec((tm, tn), lambda j, i, te: (i, j))),
        out_shape=jax.ShapeDtypeStruct((s, de), BF16),
        compiler_params=_cparams("parallel", "arbitrary"),
        name="expert_up",
    )(tile_exp, xs, wg, bg, wu, bu)


def _expert_down(tile_exp, hid, wd, bd, *, tm, tn):
    s, de = hid.shape
    d = wd.shape[2]
    return pl.pallas_call(
        _down_kernel,
        grid_spec=pltpu.PrefetchScalarGridSpec(
            num_scalar_prefetch=1,
            grid=(d // tn, s // tm),
            in_specs=[pl.BlockSpec((tm, de), lambda j, i, te: (i, 0)),
                      pl.BlockSpec((1, de, tn), lambda j, i, te: (te[i], 0, j)),
                      pl.BlockSpec((1, 1, tn), lambda j, i, te: (te[i], 0, j))],
            out_specs=pl.BlockSpec((tm, tn), lambda j, i, te: (i, j))),
        out_shape=jax.ShapeDtypeStruct((s, d), F32),
        compiler_params=_cparams("parallel", "arbitrary"),
        name="expert_down",
    )(tile_exp, hid, wd, bd)


def _combine_kernel(slot_ref, x1_ref, tw_ref, eo_ref, p_ref, gp_ref, wpg_ref, wple_ref, gf_ref, y_ref, rows_sc, sem):
    tm = x1_ref.shape[0]

    def row_copy(r, k):
        return pltpu.make_async_copy(eo_ref.at[pl.ds(slot_ref[0, 0, r * TOP_K + k], 1)],
                                     rows_sc.at[k, pl.ds(r, 1)], sem)

    def start(r, carry):
        for k in range(TOP_K):
            row_copy(r, k).start()
        return carry

    def wait(r, carry):
        for k in range(TOP_K):
            row_copy(r, k).wait()
        return carry

    lax.fori_loop(0, tm, start, 0)
    lax.fori_loop(0, tm, wait, 0)
    tw = tw_ref[...]
    moe = jnp.zeros(x1_ref.shape, F32)
    for k in range(TOP_K):
        moe = moe + tw[:, k:k + 1] * rows_sc[k].astype(F32)
    x2 = x1_ref[...] + moe
    gate = jax.nn.sigmoid(jnp.dot(_rms(x2, gp_ref[...]).astype(BF16), wpg_ref[...], preferred_element_type=F32))
    ple = jnp.dot(p_ref[...].astype(BF16), wple_ref[...], preferred_element_type=F32)
    x3 = x2 + ple * gate
    y_ref[...] = _rms(x3, gf_ref[...])


def _combine(slots, x1, topw, eo, p, g_ple, w_pg, w_ple, g_final, *, tm):
    n, d = x1.shape
    pd = p.shape[1]
    const = lambda shp: pl.BlockSpec(shp, lambda i: (0, 0), pipeline_mode=pl.Buffered(1))
    return pl.pallas_call(
        _combine_kernel,
        grid=(n // tm,),
        in_specs=[pl.BlockSpec((1, 1, tm * TOP_K), lambda i: (i, 0, 0), memory_space=pltpu.SMEM),
                  pl.BlockSpec((tm, d), lambda i: (i, 0)),
                  pl.BlockSpec((tm, LANES), lambda i: (i, 0)),
                  pl.BlockSpec(memory_space=pl.ANY),
                  pl.BlockSpec((tm, pd), lambda i: (i, 0)),
                  const((1, d)), const((d, d)), const((pd, d)), const((1, d))],
        out_specs=pl.BlockSpec((tm, d), lambda i: (i, 0)),
        out_shape=jax.ShapeDtypeStruct((n, d), F32),
        scratch_shapes=[pltpu.VMEM((TOP_K, tm, d), eo.dtype), pltpu.SemaphoreType.DMA(())],
        compiler_params=_cparams("arbitrary"),
        name="combine",
    )(slots.reshape(n // tm, 1, tm * TOP_K), x1, topw, eo, p, g_ple, w_pg, w_ple, g_final)


def _pick(n, pref):
    t = pref
    while n % t:
        t //= 2
    return t


def _layer(x, p, seqs, state, params):
    (g_mix, w_in, b_in, w_a, dw_w, dw_b, ln_g, ln_b, pw_w, pw_b, w_o, g_ffn,
     r_w, r_b, wg, bg, wu, bu, wd, bd, g_ple, w_ple, w_pg, g_final) = params
    n, d = x.shape
    heads = state[0][1].shape[1]
    dh = state[0][1].shape[2]
    mw = heads * dh
    cw = dw_w.shape[1]
    n_exp = r_w.shape[1]
    de = wg.shape[2]
    row = lambda v: v.reshape(1, -1)

    o_g = 4 * mw
    o_glu = o_g + 2 * heads
    o_gate = o_glu + 2 * cw
    w_qkvo = w_in[:, :o_g].astype(BF16)
    w_gt = jnp.zeros((d, LANES), F32).at[:, :2 * heads].set(w_in[:, o_g:o_glu])
    b_gt = jnp.zeros((1, LANES), F32).at[:, :2 * heads].set(b_in[o_g:o_glu])
    w_gt_hi, w_gt_lo = _split_bf16(w_gt)
    w_ga = w_in[:, o_glu:o_glu + cw].astype(BF16)
    w_gb = w_in[:, o_glu + cw:o_gate].astype(BF16)
    w_gab = w_in[:, o_gate:].astype(BF16)

    tm = _pick(n, 1024)
    qkvo = _norm_mm(x, row(g_mix), w_qkvo, row(b_in[:o_g]), act=None, out_dtype=BF16, tm=tm, tn=_pick(o_g, 1024))
    gab = _norm_mm(x, row(g_mix), w_gab, row(b_in[o_gate:]), act="sigmoid", out_dtype=BF16, tm=tm, tn=_pick(2 * d, 1024))
    gates, u = _gates_glu(x, row(g_mix), w_gt_hi, w_gt_lo, b_gt, w_ga, row(b_in[o_glu:o_glu + cw]),
                          w_gb, row(b_in[o_glu + cw:o_gate]), tm=tm, tn=_pick(cw, 512))

    hs_parts, bo_parts, new_state = [], [], []
    pw_bf = pw_w.astype(BF16)
    for (row0, batch, seq, zero_state), (c0, n0, m0, buf) in zip(seqs, state):
        chunk = seq
        for cand in (256, 128, 64):
            if seq % CHUNK == 0 and seq % cand == 0:
                chunk = cand
                break
        gates_r = gates[row0:row0 + batch * seq, :2 * heads].reshape(-1, chunk, 2 * heads).transpose(0, 2, 1)
        hs_i, c1, n1, m1 = _mlstm(qkvo, gates, gates_r, c0, n0, m0, row0=row0, batch=batch, seq=seq,
                                  chunk=chunk, heads=heads, dh=dh)
        taps = dw_w.shape[0]
        if zero_state:
            halo_src = u
        else:
            halo_src = jnp.pad(buf, ((0, 0), (HALO - (taps - 1), 0), (0, 0))).reshape(batch * HALO, cw)
        bo_i = _conv(u, halo_src, dw_w, row(dw_b), row(ln_g), row(ln_b), pw_bf, row(pw_b),
                     row0=row0, batch=batch, seq=seq, tt=_pick(seq, 256), zero_first=zero_state)
        u_seq = u[row0:row0 + batch * seq].reshape(batch, seq, cw)
        if seq >= taps - 1:
            buf1 = u_seq[:, seq - (taps - 1):]
        else:
            buf1 = jnp.concatenate([buf, u_seq], axis=1)[:, -(taps - 1):]
        hs_parts.append(hs_i)
        bo_parts.append(bo_i)
        new_state.append((c1, n1, m1, buf1))
    hs = jnp.concatenate(hs_parts, axis=0)
    bo = jnp.concatenate(bo_parts, axis=0)

    r_pad = jnp.zeros((d, LANES), F32).at[:, :n_exp].set(r_w)
    rb_pad = jnp.zeros((1, LANES), F32).at[:, :n_exp].set(r_b)
    r_hi, r_lo = _split_bf16(r_pad)
    x1, xn2, logits = _merge(x, hs, bo, gab, w_a.astype(BF16), w_o.astype(BF16), row(g_ffn), r_hi, r_lo, rb_pad,
                             n_exp=n_exp, tm=_pick(n, 256))
    topw, tope, rank, counts = _router(logits, tm=_pick(n, 512))

    tme = _pick(n * TOP_K, 512)
    cnt = counts[0, :n_exp].astype(I32)
    tiles_per = (cnt + tme - 1) // tme
    tile_end = jnp.cumsum(tiles_per)
    offs = (tile_end - tiles_per) * tme
    n_tiles = (n * TOP_K) // tme + n_exp
    tile_exp = jnp.minimum(jnp.searchsorted(tile_end, jnp.arange(n_tiles, dtype=I32), side="right"), n_exp - 1)
    tile_exp = jnp.concatenate([tile_exp.astype(I32), tile_end[-1:].astype(I32)])
    slots = (offs[tope[:, :TOP_K]] + rank[:, :TOP_K]).astype(I32)

    tmd = _pick(n, 256)
    xs = _dispatch(xn2, slots, n_tiles * tme, tm=tmd)
    hid = _expert_up(tile_exp, xs, wg.astype(BF16), bg[:, None, :], wu.astype(BF16), bu[:, None, :],
                     tm=tme, tn=_pick(de, 1024))
    eo = _expert_down(tile_exp, hid, wd.astype(BF16), bd[:, None, :], tm=tme, tn=_pick(d, 1024))
    y = _combine(slots, x1, topw, eo, p, row(g_ple), w_pg.astype(BF16), w_ple.astype(BF16), row(g_final), tm=tmd)
    return y, new_state


def kernel(x_prompt, x_sample, state_mlstm_C, state_mlstm_n, state_mlstm_m, cache_conv, p_prompt, p_sample, norm_mix, w_in, b_in, w_a, conv_dw_w, conv_dw_b, conv_ln_g, conv_ln_b, conv_pw_w, conv_pw_b, w_o, norm_ffn, router_w, router_b, exp_w_gate, exp_b_gate, exp_w_up, exp_b_up, exp_w_down, exp_b_down, norm_ple, w_ple, w_ple_gate, norm_final):
    depth = w_in.shape[0]
    assert depth == 1, "the final RMSNorm is fused into the single layer's last kernel"
    bp, sp, d = x_prompt.shape
    bs, ss, _ = x_sample.shape
    heads, dh = state_mlstm_C.shape[2], state_mlstm_C.shape[3]
    cw = cache_conv.shape[3]
    taps = conv_dw_w.shape[1]
    n_p, n_s = bp * sp, bs * ss
    x = jnp.concatenate([x_prompt.reshape(n_p, d), x_sample.reshape(n_s, d)], axis=0)
    l = 0
    p = jnp.concatenate([p_prompt[l].reshape(n_p, -1), p_sample[l].reshape(n_s, -1)], axis=0)
    zero_state = (jnp.zeros((bp, heads, dh, dh), F32), jnp.zeros((bp, heads, dh), F32),
                  jnp.zeros((bp, heads), F32), jnp.zeros((bp, taps - 1, cw), F32))
    carried = (state_mlstm_C[l].astype(F32), state_mlstm_n[l].astype(F32), state_mlstm_m[l].astype(F32), cache_conv[l])
    params = (norm_mix[l], w_in[l], b_in[l], w_a[l], conv_dw_w[l], conv_dw_b[l], conv_ln_g[l], conv_ln_b[l],
              conv_pw_w[l], conv_pw_b[l], w_o[l], norm_ffn[l], router_w[l], router_b[l],
              exp_w_gate[l], exp_b_gate[l], exp_w_up[l], exp_b_up[l], exp_w_down[l], exp_b_down[l],
              norm_ple[l], w_ple[l], w_ple_gate[l], norm_final)
    seqs = [(0, bp, sp, True), (n_p, bs, ss, False)]
    y, ((c_p, n_pp, m_p, b_p), (c_s, n_sp, m_s, b_s)) = _layer(x, p, seqs, [zero_state, carried], params)
    y_prompt = y[:n_p].reshape(bp, sp, d)
    y_sample = y[n_p:].reshape(bs, ss, d)
    st = lambda a: a[None]
    return (y_prompt, y_sample, st(c_p), st(n_pp), st(m_p), st(b_p), st(c_s), st(n_sp), st(m_s), st(b_s))
```

```python
import functools
import math

import jax
import jax.numpy as jnp
from jax import lax
from jax.experimental import pallas as pl
from jax.experimental.pallas import tpu as pltpu

F32 = jnp.float32
BF16 = jnp.bfloat16
I32 = jnp.int32
U32 = jnp.uint32

RMS_EPS = 1e-6
LN_EPS = 1e-5
TOP_K = 4
SWIGLU_ALPHA = 1.702
SWIGLU_LIMIT = 7.0
CHUNK = 64
LANES = 128
SUBLANES = 8
HALO = 32
VMEM_LIMIT = 56 * 1024 * 1024


def _cparams(*sem):
    return pltpu.CompilerParams(dimension_semantics=sem, vmem_limit_bytes=VMEM_LIMIT)


def _rms(x, g):
    ms = jnp.mean(x * x, axis=-1, keepdims=True)
    return x * lax.rsqrt(ms + RMS_EPS) * g


def _split_bf16(x):
    hi = x.astype(BF16)
    lo = (x - hi.astype(F32)).astype(BF16)
    return hi, lo


def _pack_rows(x, groups):
    w = x.shape[1] // groups
    out = []
    for g in range(groups):
        lo = lax.bitcast_convert_type(x[:, g * w:g * w + w // 2].astype(BF16).astype(F32), U32)
        hi = lax.bitcast_convert_type(x[:, g * w + w // 2:(g + 1) * w].astype(BF16).astype(F32), U32)
        out.append((lo >> 16) | hi)
    return out[0] if groups == 1 else jnp.concatenate(out, axis=1)


def _unpack_rows(p, groups):
    w = p.shape[1] // groups
    out = []
    for g in range(groups):
        pg = p[:, g * w:(g + 1) * w]
        out.append(lax.bitcast_convert_type(pg << 16, F32))
        out.append(lax.bitcast_convert_type(pg & jnp.uint32(0xFFFF0000), F32))
    return jnp.concatenate(out, axis=1)


def _two_part(tm, d, n_a_tiles, ngrid):
    if ngrid == 1:
        return [pl.BlockSpec((tm, d), lambda i: (jnp.minimum(i, n_a_tiles - 1), 0)),
                pl.BlockSpec((tm, d), lambda i: (jnp.maximum(i - n_a_tiles, 0), 0), pipeline_mode=pl.Buffered(1))]
    return [pl.BlockSpec((tm, d), lambda i, j: (jnp.minimum(i, n_a_tiles - 1), 0)),
            pl.BlockSpec((tm, d), lambda i, j: (jnp.maximum(i - n_a_tiles, 0), 0), pipeline_mode=pl.Buffered(1))]


def _rows(i, n_a_tiles, a_ref, b_ref):
    return jnp.where(i < n_a_tiles, a_ref[...], b_ref[...])


def _norm_mm_kernel(xa_ref, xb_ref, g_ref, w_ref, b_ref, o_ref, xn_ref, *, act, n_a_tiles):
    i = pl.program_id(0)

    @pl.when(pl.program_id(1) == 0)
    def _():
        xn_ref[...] = _rms(_rows(i, n_a_tiles, xa_ref, xb_ref), g_ref[...]).astype(BF16)

    acc = jnp.dot(xn_ref[...], w_ref[...], preferred_element_type=F32) + b_ref[...]
    if act == "sigmoid":
        acc = jax.nn.sigmoid(acc)
    o_ref[...] = acc.astype(o_ref.dtype)


def _norm_mm(xa, xb, g, w, b, *, act, out_dtype, tm, tn):
    d = xa.shape[1]
    n = xa.shape[0] + xb.shape[0]
    n_a_tiles = xa.shape[0] // tm
    cols = w.shape[1]
    return pl.pallas_call(
        functools.partial(_norm_mm_kernel, act=act, n_a_tiles=n_a_tiles),
        grid=(n // tm, cols // tn),
        in_specs=_two_part(tm, d, n_a_tiles, 2) + [
            pl.BlockSpec((1, d), lambda i, j: (0, 0)),
            pl.BlockSpec((d, tn), lambda i, j: (0, j)),
            pl.BlockSpec((1, tn), lambda i, j: (0, j))],
        out_specs=pl.BlockSpec((tm, tn), lambda i, j: (i, j)),
        out_shape=jax.ShapeDtypeStruct((n, cols), out_dtype),
        scratch_shapes=[pltpu.VMEM((tm, d), BF16)],
        compiler_params=_cparams("parallel", "arbitrary"),
        name="norm_mm_" + str(act),
    )(xa, xb, g, w, b)


def _gates_glu_kernel(xa_ref, xb_ref, g_ref, wgh_ref, wgl_ref, bg_ref, wa_ref, ba_ref, wb_ref, bb_ref,
                      gate_ref, u_ref, xh_ref, *, n_a_tiles):
    i = pl.program_id(0)

    @pl.when(pl.program_id(1) == 0)
    def _():
        xn = _rms(_rows(i, n_a_tiles, xa_ref, xb_ref), g_ref[...])
        hi, lo = _split_bf16(xn)
        xh_ref[...] = hi
        gate_ref[...] = (jnp.dot(hi, wgh_ref[...], preferred_element_type=F32)
                         + jnp.dot(hi, wgl_ref[...], preferred_element_type=F32)
                         + jnp.dot(lo, wgh_ref[...], preferred_element_type=F32)
                         + bg_ref[...])

    xh = xh_ref[...]
    a = jnp.dot(xh, wa_ref[...], preferred_element_type=F32) + ba_ref[...]
    b = jnp.dot(xh, wb_ref[...], preferred_element_type=F32) + bb_ref[...]
    u_ref[...] = a * jax.nn.sigmoid(b)


def _gates_glu(xa, xb, g, wgh, wgl, bg, wa, ba, wb, bb, *, tm, tn):
    d = xa.shape[1]
    n = xa.shape[0] + xb.shape[0]
    n_a_tiles = xa.shape[0] // tm
    cw = wa.shape[1]
    return pl.pallas_call(
        functools.partial(_gates_glu_kernel, n_a_tiles=n_a_tiles),
        grid=(n // tm, cw // tn),
        in_specs=_two_part(tm, d, n_a_tiles, 2) + [
            pl.BlockSpec((1, d), lambda i, j: (0, 0)),
            pl.BlockSpec((d, LANES), lambda i, j: (0, 0)),
            pl.BlockSpec((d, LANES), lambda i, j: (0, 0)),
            pl.BlockSpec((1, LANES), lambda i, j: (0, 0)),
            pl.BlockSpec((d, tn), lambda i, j: (0, j)),
            pl.BlockSpec((1, tn), lambda i, j: (0, j)),
            pl.BlockSpec((d, tn), lambda i, j: (0, j)),
            pl.BlockSpec((1, tn), lambda i, j: (0, j))],
        out_specs=[pl.BlockSpec((tm, LANES), lambda i, j: (i, 0)),
                   pl.BlockSpec((tm, tn), lambda i, j: (i, j))],
        out_shape=[jax.ShapeDtypeStruct((n, LANES), F32),
                   jax.ShapeDtypeStruct((n, cw), F32)],
        scratch_shapes=[pltpu.VMEM((tm, d), BF16)],
        compiler_params=_cparams("parallel", "arbitrary"),
        name="gates_glu",
    )(xa, xb, g, wgh, wgl, bg, wa, ba, wb, bb)


def _mlstm_kernel(q_ref, k_ref, v_ref, o_ref, gc_ref, gr_ref, c0_ref, n0_ref, m0_ref,
                  hs_ref, c1_ref, n1_ref, m1_ref, c_sc, n_sc, m_sc, *, heads, dh, scale):
    c = pl.program_id(1)
    nc = pl.num_programs(1)
    L = q_ref.shape[0]

    @pl.when(c == 0)
    def _():
        c_sc[...] = c0_ref[0]
        n_sc[...] = n0_ref[0]
        m_sc[...] = m0_ref[0]

    row = lax.broadcasted_iota(I32, (L, L), 0)
    col = lax.broadcasted_iota(I32, (L, L), 1)
    causal = col <= row
    gc = gc_ref[...]
    gr = gr_ref[0]

    for h in range(heads):
        sl = slice(h * dh, (h + 1) * dh)
        q = q_ref[:, sl]
        ks = k_ref[:, sl] * scale
        v = v_ref[:, sl]
        ig_col = gc[:, h:h + 1]
        lf_col = jax.nn.log_sigmoid(gc[:, heads + h:heads + h + 1])
        ig_row = gr[h:h + 1, :]
        lf_row = jax.nn.log_sigmoid(gr[heads + h:heads + h + 1, :])
        f_col = jnp.sum(jnp.where(causal, lf_row, 0.0), axis=1, keepdims=True)
        f_row = jnp.sum(jnp.where(row <= col, lf_col, 0.0), axis=0, keepdims=True)
        m0 = m_sc[h]
        n0 = n_sc[h]
        c0 = c_sc[h]
        dmat = jnp.where(causal, f_col - f_row + ig_row, -jnp.inf)
        inter = f_col + m0
        m_col = jnp.maximum(inter, jnp.max(dmat, axis=1, keepdims=True))
        a_col = jnp.exp(inter - m_col)
        s = lax.dot_general(q, ks, (((1,), (1,)), ((), ())), preferred_element_type=F32)
        w = jnp.exp(dmat - m_col) * s
        qc = lax.dot_general(q, c0.astype(BF16), (((1,), (1,)), ((), ())), preferred_element_type=F32)
        num = a_col * qc + jnp.dot(w.astype(BF16), v, preferred_element_type=F32)
        qf = q.astype(F32)
        den_raw = a_col * jnp.sum(qf * n0, axis=1, keepdims=True) + jnp.sum(w, axis=1, keepdims=True)
        den = jnp.maximum(jnp.abs(den_raw), jnp.exp(-m_col))
        hval = num / den
        hs_ref[:, sl] = (jax.nn.sigmoid(o_ref[:, sl].astype(F32)) * hval).astype(hs_ref.dtype)
        m_last = m_col[L - 1:L, :]
        f_last = f_col[L - 1:L, :]
        w_end = jnp.exp(f_last - f_col + ig_col - m_last)
        decay = jnp.exp(inter[L - 1:L, :] - m_last)
        vw = (v.astype(F32) * w_end).astype(BF16)
        c_sc[h] = decay * c0 + lax.dot_general(vw, ks, (((0,), (0,)), ((), ())), preferred_element_type=F32)
        n_sc[h] = decay * n0 + jnp.sum(ks.astype(F32) * w_end, axis=0, keepdims=True)
        m_sc[h] = m_last

    @pl.when(c == nc - 1)
    def _():
        c1_ref[0] = c_sc[...]
        n1_ref[0] = n_sc[...]
        m1_ref[0] = m_sc[...]


def _mlstm(qkvo, gates_c, gates_r, c0, n0, m0, *, row0, batch, seq, chunk, heads, dh):
    width = heads * dh
    nc = seq // chunk
    rb0 = row0 // chunk
    n0r = n0.reshape(batch, heads, 1, dh)
    m0r = m0.reshape(batch, heads, 1, 1)

    def blk(part):
        return pl.BlockSpec((chunk, width), lambda b, c: (rb0 + b * nc + c, part))

    state_spec = lambda shp: pl.BlockSpec((1,) + shp, lambda b, c: (b, 0, 0, 0))
    hs, c1, n1, m1 = pl.pallas_call(
        functools.partial(_mlstm_kernel, heads=heads, dh=dh, scale=dh ** -0.5),
        grid=(batch, nc),
        in_specs=[blk(0), blk(1), blk(2), blk(3),
                  pl.BlockSpec((chunk, LANES), lambda b, c: (rb0 + b * nc + c, 0)),
                  pl.BlockSpec((1, 2 * heads, chunk), lambda b, c: (b * nc + c, 0, 0)),
                  state_spec((heads, dh, dh)), state_spec((heads, 1, dh)), state_spec((heads, 1, 1))],
        out_specs=[pl.BlockSpec((chunk, width), lambda b, c: (b * nc + c, 0)),
                   state_spec((heads, dh, dh)), state_spec((heads, 1, dh)), state_spec((heads, 1, 1))],
        out_shape=[jax.ShapeDtypeStruct((batch * seq, width), BF16),
                   jax.ShapeDtypeStruct((batch, heads, dh, dh), F32),
                   jax.ShapeDtypeStruct((batch, heads, 1, dh), F32),
                   jax.ShapeDtypeStruct((batch, heads, 1, 1), F32)],
        scratch_shapes=[pltpu.VMEM((heads, dh, dh), F32), pltpu.VMEM((heads, 1, dh), F32),
                        pltpu.VMEM((heads, 1, 1), F32)],
        compiler_params=_cparams("parallel", "arbitrary"),
        name="mlstm",
    )(qkvo, qkvo, qkvo, qkvo, gates_c, gates_r, c0, n0r, m0r)
    return hs, c1, n1.reshape(batch, heads, dh), m1.reshape(batch, heads)


def _conv_kernel(halo_ref, u_ref, dw_ref, dwb_ref, lg_ref, lb_ref, pw_ref, pwb_ref, o_ref, full_sc, sh_sc, wb_sc, y_sc,
                 *, taps, zero_first, rb, cb):
    tt, cw = u_ref.shape
    halo = halo_ref[...]
    if zero_first:
        halo = jnp.where(pl.program_id(1) == 0, 0.0, halo)
    full_sc[0:HALO, :] = halo
    full_sc[HALO:HALO + tt, :] = u_ref[...]
    span = tt + HALO - SUBLANES
    for s in range(1, SUBLANES):
        sh_sc[s - 1] = full_sc[s:s + span, :]
    for j in range(taps):
        wb_sc[j] = jnp.broadcast_to(dw_ref[j:j + 1, :], (SUBLANES, cw))
    base = HALO - (taps - 1)
    n_acc = 1

    for c0 in range(0, cw, cb):
        cols = slice(c0, c0 + cb)

        def body(r, carry):
            r0 = pl.multiple_of(r * rb, rb)
            accs = [jnp.zeros((rb, cb), F32) + dwb_ref[:, cols]] + [None] * (n_acc - 1)
            for j in range(taps):
                a, s = divmod(base + j, SUBLANES)
                rows = pl.ds(r0 + a * SUBLANES, rb)
                src = full_sc[rows, cols] if s == 0 else sh_sc[s - 1, rows, cols]
                term = jnp.concatenate([wb_sc[j, :, cols]] * (rb // SUBLANES), axis=0) * src
                accs[j % n_acc] = term if accs[j % n_acc] is None else accs[j % n_acc] + term
            y_sc[pl.ds(r0, rb), cols] = functools.reduce(lambda p, q: p + q, accs)
            return carry

        lax.fori_loop(0, tt // rb, body, 0)
    y = y_sc[...]
    mu = jnp.mean(y, axis=-1, keepdims=True)
    yc = y - mu
    yn = yc * lax.rsqrt(jnp.mean(yc * yc, axis=-1, keepdims=True) + LN_EPS) * lg_ref[...] + lb_ref[...]
    act = yn * jax.nn.sigmoid(yn)
    o_ref[...] = (jnp.dot(act.astype(BF16), pw_ref[...], preferred_element_type=F32) + pwb_ref[...]).astype(o_ref.dtype)


def _conv(u, halo_src, dw_w, dw_b, ln_g, ln_b, pw_w, pw_b, *, row0, batch, seq, tt, zero_first):
    cw = u.shape[1]
    d = pw_w.shape[1]
    taps = dw_w.shape[0]
    nt = seq // tt
    tb0 = row0 // tt
    if zero_first:
        hb0, per = row0 // HALO, tt // HALO
        halo_map = lambda b, t: (jnp.maximum(hb0 + (b * nt + t) * per - 1, 0), 0)
    else:
        assert nt == 1, "a carried conv cache is only supported for single-tile sequences"
        halo_map = lambda b, t: (b, 0)
    dw_pad = jnp.zeros((HALO, cw), F32).at[:taps].set(dw_w)
    return pl.pallas_call(
        functools.partial(_conv_kernel, taps=taps, zero_first=zero_first, rb=min(4 * SUBLANES, tt), cb=min(2 * LANES, cw)),
        grid=(batch, nt),
        in_specs=[pl.BlockSpec((HALO, cw), halo_map),
                  pl.BlockSpec((tt, cw), lambda b, t: (tb0 + b * nt + t, 0)),
                  pl.BlockSpec((HALO, cw), lambda b, t: (0, 0)),
                  pl.BlockSpec((1, cw), lambda b, t: (0, 0)),
                  pl.BlockSpec((1, cw), lambda b, t: (0, 0)),
                  pl.BlockSpec((1, cw), lambda b, t: (0, 0)),
                  pl.BlockSpec((cw, d), lambda b, t: (0, 0)),
                  pl.BlockSpec((1, d), lambda b, t: (0, 0))],
        out_specs=pl.BlockSpec((tt, d), lambda b, t: (b * nt + t, 0)),
        out_shape=jax.ShapeDtypeStruct((batch * seq, d), BF16),
        scratch_shapes=[pltpu.VMEM((HALO + tt, cw), F32),
                        pltpu.VMEM((SUBLANES - 1, HALO + tt - SUBLANES, cw), F32),
                        pltpu.VMEM((taps, SUBLANES, cw), F32),
                        pltpu.VMEM((tt, cw), F32)],
        compiler_params=_cparams("parallel", "arbitrary"),
        name="conv_module",
    )(halo_src, u, dw_pad, dw_b, ln_g, ln_b, pw_w, pw_b)


def _merge_kernel(xa_ref, xb_ref, hsa_ref, hsb_ref, boa_ref, bob_ref, ga_ref, gb_ref, wa_ref, wo_ref, g_ref,
                  rwh_ref, rwl_ref, rb_ref, x1_ref, xn_ref, lg_ref, *, n_exp, n_a_tiles):
    i = pl.program_id(0)
    a = jnp.dot(_rows(i, n_a_tiles, hsa_ref, hsb_ref), wa_ref[...], preferred_element_type=F32)
    bo = _rows(i, n_a_tiles, boa_ref, bob_ref)
    mix = ga_ref[...].astype(F32) * a + gb_ref[...].astype(F32) * bo.astype(F32)
    x1 = _rows(i, n_a_tiles, xa_ref, xb_ref) + jnp.dot(mix.astype(BF16), wo_ref[...], preferred_element_type=F32)
    x1_ref[...] = x1
    xn = _rms(x1, g_ref[...])
    xn_ref[...] = _pack_rows(xn, 1)
    hi, lo = _split_bf16(xn)
    logits = (jnp.dot(hi, rwh_ref[...], preferred_element_type=F32)
              + jnp.dot(hi, rwl_ref[...], preferred_element_type=F32)
              + jnp.dot(lo, rwh_ref[...], preferred_element_type=F32)
              + rb_ref[...])
    lane = lax.broadcasted_iota(I32, logits.shape, 1)
    lg_ref[...] = jnp.where(lane < n_exp, logits, -jnp.inf)


def _merge(xa, xb, hsa, hsb, boa, bob, gab, w_a, w_o, g, rwh, rwl, rb, *, n_exp, tm):
    d = xa.shape[1]
    n = xa.shape[0] + xb.shape[0]
    n_a_tiles = xa.shape[0] // tm
    mw = hsa.shape[1]
    const = lambda shp: pl.BlockSpec(shp, lambda i: (0, 0), pipeline_mode=pl.Buffered(1))
    return pl.pallas_call(
        functools.partial(_merge_kernel, n_exp=n_exp, n_a_tiles=n_a_tiles),
        grid=(n // tm,),
        in_specs=_two_part(tm, d, n_a_tiles, 1) + _two_part(tm, mw, n_a_tiles, 1) + _two_part(tm, d, n_a_tiles, 1) + [
            pl.BlockSpec((tm, d), lambda i: (i, 0)),
            pl.BlockSpec((tm, d), lambda i: (i, 1)),
            const((mw, d)), const((d, d)), const((1, d)),
            const((d, LANES)), const((d, LANES)), const((1, LANES))],
        out_specs=[pl.BlockSpec((tm, d), lambda i: (i, 0)),
                   pl.BlockSpec((tm, d // 2), lambda i: (i, 0)),
                   pl.BlockSpec((tm, LANES), lambda i: (i, 0))],
        out_shape=[jax.ShapeDtypeStruct((n, d), F32),
                   jax.ShapeDtypeStruct((n, d // 2), U32),
                   jax.ShapeDtypeStruct((n, LANES), F32)],
        compiler_params=_cparams("parallel"),
        name="merge",
    )(xa, xb, hsa, hsb, boa, bob, gab, gab, w_a, w_o, g, rwh, rwl, rb)


def _router_kernel(lg_ref, w_ref, e_ref, r_ref, cnt_ref, carry_sc):
    i = pl.program_id(0)

    @pl.when(i == 0)
    def _():
        carry_sc[...] = jnp.zeros_like(carry_sc)

    l = lg_ref[...]
    tm = l.shape[0]
    lane = lax.broadcasted_iota(I32, l.shape, 1)
    lane_f = lane.astype(F32)
    vals, hots, idxs = [], [], []
    for _ in range(TOP_K):
        mx = jnp.max(l, axis=1, keepdims=True)
        idx = jnp.min(jnp.where(l == mx, lane_f, float(LANES)), axis=1, keepdims=True)
        hot = lane_f == idx
        vals.append(mx)
        hots.append(hot)
        idxs.append(idx)
        l = jnp.where(hot, -jnp.inf, l)
    ex = [jnp.exp(v - vals[0]) for v in vals]
    tot = ex[0] + ex[1] + ex[2] + ex[3]
    sel = jnp.zeros(l.shape, F32)
    for hot in hots:
        sel = jnp.where(hot, 1.0, sel)
    rr = lax.broadcasted_iota(I32, (tm, tm), 0)
    cc = lax.broadcasted_iota(I32, (tm, tm), 1)
    tri = jnp.where(cc < rr, 1.0, 0.0).astype(BF16)
    cum = jnp.dot(tri, sel.astype(BF16), preferred_element_type=F32) + carry_sc[...]
    w_out = jnp.zeros(l.shape, F32)
    e_out = jnp.zeros(l.shape, I32)
    r_out = jnp.zeros(l.shape, I32)
    for k in range(TOP_K):
        rank = jnp.sum(jnp.where(hots[k], cum, 0.0), axis=1, keepdims=True).astype(I32)
        w_out = jnp.where(lane == k, ex[k] / tot, w_out)
        e_out = jnp.where(lane == k, idxs[k].astype(I32), e_out)
        r_out = jnp.where(lane == k, rank, r_out)
    w_ref[...] = w_out
    e_ref[...] = e_out
    r_ref[...] = r_out
    carry_sc[...] = carry_sc[...] + jnp.sum(sel, axis=0, keepdims=True)
    cnt_ref[...] = carry_sc[...]


def _router(logits, *, tm):
    n = logits.shape[0]
    blk = pl.BlockSpec((tm, LANES), lambda i: (i, 0))
    return pl.pallas_call(
        _router_kernel,
        grid=(n // tm,),
        in_specs=[blk],
        out_specs=[blk, blk, blk, pl.BlockSpec((1, LANES), lambda i: (0, 0))],
        out_shape=[jax.ShapeDtypeStruct((n, LANES), F32), jax.ShapeDtypeStruct((n, LANES), I32),
                   jax.ShapeDtypeStruct((n, LANES), I32), jax.ShapeDtypeStruct((1, LANES), F32)],
        scratch_shapes=[pltpu.VMEM((1, LANES), F32)],
        compiler_params=_cparams("arbitrary"),
        name="router",
    )(logits)


def _dispatch_kernel(pad_ref, slot_ref, x_ref, xs_ref, zero_sc, sem, zsem, *, n_exp, n_tiles):
    tm = x_ref.shape[0]
    tme = zero_sc.shape[0]

    def zero_fill(wait):
        def run(row, size):
            cp = pltpu.make_async_copy(zero_sc.at[pl.ds(0, size)], xs_ref.at[pl.ds(row, size)], zsem)
            if wait:
                cp.wait()
            else:
                cp.start()

        def per_expert(e, carry):
            first, length = pad_ref[e], pad_ref[n_exp + e]
            head = jnp.minimum((-first) & (SUBLANES - 1), length)
            lax.fori_loop(0, head, lambda r, c: (run(first + r, 1), c)[1], 0)
            body0, body = first + head, length - head
            size = tme // 2
            while size >= SUBLANES:
                @pl.when((body & size) != 0)
                def _(size=size):
                    run(pl.multiple_of(body0 + (body & ~(2 * size - 1)), SUBLANES), size)
                size //= 2
            return carry

        def per_tile(t, carry):
            run(pl.multiple_of(t * tme, tme), tme)
            return carry

        lax.fori_loop(0, n_exp, per_expert, 0)
        lax.fori_loop(pad_ref[2 * n_exp], n_tiles, per_tile, 0)

    @pl.when(pl.program_id(0) == 0)
    def _():
        zero_sc[...] = jnp.zeros_like(zero_sc)
        zero_fill(wait=False)
        zero_fill(wait=True)

    def row_copy(r, k):
        return pltpu.make_async_copy(x_ref.at[pl.ds(r, 1)], xs_ref.at[pl.ds(slot_ref[0, 0, r * TOP_K + k], 1)], sem)

    def start_rows(r, carry):
        for k in range(TOP_K):
            row_copy(r, k).start(priority=k % 2)
        return carry

    def wait_rows(r, carry):
        for k in range(TOP_K):
            row_copy(r, k).wait()
        return carry

    lax.fori_loop(0, tm, start_rows, 0)
    lax.fori_loop(0, tm, wait_rows, 0)


def _dispatch(pads, slots, xn, n_tiles, tme, *, n_exp, tm):
    n, dw = xn.shape
    n_slots = n_tiles * tme
    return pl.pallas_call(
        functools.partial(_dispatch_kernel, n_exp=n_exp, n_tiles=n_tiles),
        grid_spec=pltpu.PrefetchScalarGridSpec(
            num_scalar_prefetch=1,
            grid=(n // tm,),
            in_specs=[pl.BlockSpec((1, 1, tm * TOP_K), lambda i, pads: (i, 0, 0), memory_space=pltpu.SMEM),
                      pl.BlockSpec((tm, dw), lambda i, pads: (i, 0))],
            out_specs=pl.BlockSpec(memory_space=pl.ANY),
            scratch_shapes=[pltpu.VMEM((tme, dw), xn.dtype), pltpu.SemaphoreType.DMA(()),
                            pltpu.SemaphoreType.DMA(())]),
        out_shape=jax.ShapeDtypeStruct((n_slots, dw), xn.dtype),
        compiler_params=_cparams("arbitrary"),
        name="dispatch",
    )(pads, slots.reshape(n // tm, 1, tm * TOP_K), xn)


def _tile_state(te_ref):
    i = pl.program_id(1)
    used = i < te_ref[pl.num_programs(1)]
    new_weights = jnp.logical_or(i == 0, te_ref[i] != te_ref[jnp.maximum(i - 1, 0)])
    return used, new_weights


def _up_kernel(te_ref, xs_ref, wg_ref, bg_ref, wu_ref, bu_ref, h_ref, wg_sc, wu_sc):
    used, new_weights = _tile_state(te_ref)

    @pl.when(jnp.logical_and(used, new_weights))
    def _():
        wg_sc[...] = wg_ref[0].astype(BF16)
        wu_sc[...] = wu_ref[0].astype(BF16)

    @pl.when(used)
    def _():
        x = _unpack_rows(xs_ref[...], 1).astype(BF16)
        g = jnp.minimum(jnp.dot(x, wg_sc[...], preferred_element_type=F32) + bg_ref[0], SWIGLU_LIMIT)
        u = jnp.clip(jnp.dot(x, wu_sc[...], preferred_element_type=F32) + bu_ref[0], -SWIGLU_LIMIT, SWIGLU_LIMIT)
        h_ref[...] = (g * jax.nn.sigmoid(SWIGLU_ALPHA * g) * (u + 1.0)).astype(h_ref.dtype)

    @pl.when(jnp.logical_not(used))
    def _():
        h_ref[...] = jnp.zeros_like(h_ref)


def _down_kernel(te_ref, h_ref, wd_ref, bd_ref, o_ref, wd_sc):
    used, new_weights = _tile_state(te_ref)

    @pl.when(jnp.logical_and(used, new_weights))
    def _():
        wd_sc[...] = wd_ref[0].astype(BF16)

    @pl.when(used)
    def _():
        o_ref[...] = _pack_rows(jnp.dot(h_ref[...], wd_sc[...], preferred_element_type=F32) + bd_ref[0], 1)

    @pl.when(jnp.logical_not(used))
    def _():
        o_ref[...] = jnp.zeros_like(o_ref)


def _used_tile(i, te):
    return jnp.where(i < te[te.shape[0] - 1], i, 0)


def _expert_up(tile_exp, xs, wg, bg, wu, bu, *, tm, tn):
    s, dw = xs.shape
    d, de = wg.shape[1], wg.shape[2]
    wspec = pl.BlockSpec((1, d, tn), lambda j, i, te: (te[i], 0, j))
    bspec = pl.BlockSpec((1, 1, tn), lambda j, i, te: (te[i], 0, j))
    return pl.pallas_call(
        _up_kernel,
        grid_spec=pltpu.PrefetchScalarGridSpec(
            num_scalar_prefetch=1,
            grid=(de // tn, s // tm),
            in_specs=[pl.BlockSpec((tm, dw), lambda j, i, te: (_used_tile(i, te), 0)), wspec, bspec, wspec, bspec],
            out_specs=pl.BlockSpec((tm, tn), lambda j, i, te: (i, j)),
            scratch_shapes=[pltpu.VMEM((d, tn), BF16), pltpu.VMEM((d, tn), BF16)]),
        out_shape=jax.ShapeDtypeStruct((s, de), BF16),
        compiler_params=_cparams("parallel", "arbitrary"),
        name="expert_up",
    )(tile_exp, xs, wg, bg, wu, bu)


def _expert_down(tile_exp, hid, wd, bd, *, tm, tn):
    s, de = hid.shape
    d = wd.shape[2]
    return pl.pallas_call(
        _down_kernel,
        grid_spec=pltpu.PrefetchScalarGridSpec(
            num_scalar_prefetch=1,
            grid=(d // tn, s // tm),
            in_specs=[pl.BlockSpec((tm, de), lambda j, i, te: (_used_tile(i, te), 0)),
                      pl.BlockSpec((1, de, tn), lambda j, i, te: (te[i], 0, j)),
                      pl.BlockSpec((1, 1, tn), lambda j, i, te: (te[i], 0, j))],
            out_specs=pl.BlockSpec((tm, tn // 2), lambda j, i, te: (i, j)),
            scratch_shapes=[pltpu.VMEM((de, tn), BF16)]),
        out_shape=jax.ShapeDtypeStruct((s, d // 2), U32),
        compiler_params=_cparams("parallel", "arbitrary"),
        name="expert_down",
    )(tile_exp, hid, wd, bd)


def _combine_kernel(slot_ref, nslot_ref, x1_ref, tw_ref, eo_ref, p_ref, gp_ref, wpg_ref, wple_ref, gf_ref, y_ref,
                    rows_sc, sem, *, groups):
    tm = x1_ref.shape[0]
    i = pl.program_id(0)
    nt = pl.num_programs(0)

    def row_copy(ref, buf, r, k):
        return pltpu.make_async_copy(eo_ref.at[pl.ds(ref[0, 0, r * TOP_K + k], 1)],
                                     rows_sc.at[buf, k, pl.ds(r, 1)], sem.at[buf])

    def start_tile(ref, buf):
        def body(r, carry):
            for k in range(TOP_K):
                row_copy(ref, buf, r, k).start(priority=k % 2)
            return carry
        lax.fori_loop(0, tm, body, 0)

    @pl.when(i == 0)
    def _():
        start_tile(slot_ref, 0)

    @pl.when(i + 1 < nt)
    def _():
        start_tile(nslot_ref, (i + 1) % 2)

    buf = i % 2

    def wait_body(r, carry):
        for k in range(TOP_K):
            row_copy(slot_ref, buf, r, k).wait()
        return carry

    lax.fori_loop(0, tm, wait_body, 0)
    tw = tw_ref[...]
    moe = jnp.zeros(x1_ref.shape, F32)
    for k in range(TOP_K):
        moe = moe + tw[:, k:k + 1] * _unpack_rows(rows_sc[buf, k], groups)
    x2 = x1_ref[...] + moe
    gate = jax.nn.sigmoid(jnp.dot(_rms(x2, gp_ref[...]).astype(BF16), wpg_ref[...], preferred_element_type=F32))
    ple = jnp.dot(p_ref[...].astype(BF16), wple_ref[...], preferred_element_type=F32)
    x3 = x2 + ple * gate
    y_ref[...] = _rms(x3, gf_ref[...])


def _combine(slots, x1, topw, eo, p, g_ple, w_pg, w_ple, g_final, *, row0, tm, groups):
    n, d = x1.shape
    n_part, pd = p.shape
    dw = eo.shape[1]
    nt = n_part // tm
    t0 = row0 // tm
    const = lambda shp: pl.BlockSpec(shp, lambda i: (0, 0), pipeline_mode=pl.Buffered(1))
    slots3 = slots.reshape(n // tm, 1, tm * TOP_K)
    return pl.pallas_call(
        functools.partial(_combine_kernel, groups=groups),
        grid=(nt,),
        in_specs=[pl.BlockSpec((1, 1, tm * TOP_K), lambda i: (t0 + i, 0, 0), memory_space=pltpu.SMEM),
                  pl.BlockSpec((1, 1, tm * TOP_K), lambda i: (t0 + jnp.minimum(i + 1, nt - 1), 0, 0),
                               memory_space=pltpu.SMEM),
                  pl.BlockSpec((tm, d), lambda i: (t0 + i, 0)),
                  pl.BlockSpec((tm, LANES), lambda i: (t0 + i, 0)),
                  pl.BlockSpec(memory_space=pl.ANY),
                  pl.BlockSpec((tm, pd), lambda i: (i, 0)),
                  const((1, d)), const((d, d)), const((pd, d)), const((1, d))],
        out_specs=pl.BlockSpec((tm, d), lambda i: (i, 0)),
        out_shape=jax.ShapeDtypeStruct((n_part, d), F32),
        scratch_shapes=[pltpu.VMEM((2, TOP_K, tm, dw), eo.dtype), pltpu.SemaphoreType.DMA((2,))],
        compiler_params=_cparams("arbitrary"),
        name="combine",
    )(slots3, slots3, x1, topw, eo, p, g_ple, w_pg, w_ple, g_final)


def _pick(n, pref):
    t = pref
    while n % t:
        t //= 2
    return t


def _layer(xa, xb, pa, pb, seqs, state, params):
    (g_mix, w_in, b_in, w_a, dw_w, dw_b, ln_g, ln_b, pw_w, pw_b, w_o, g_ffn,
     r_w, r_b, wg, bg, wu, bu, wd, bd, g_ple, w_ple, w_pg, g_final) = params
    d = xa.shape[1]
    n = xa.shape[0] + xb.shape[0]
    both = math.gcd(xa.shape[0], xb.shape[0])
    heads = state[0][1].shape[1]
    dh = state[0][1].shape[2]
    mw = heads * dh
    cw = dw_w.shape[1]
    taps = dw_w.shape[0]
    n_exp = r_w.shape[1]
    de = wg.shape[2]
    row = lambda v: v.reshape(1, -1)

    o_g = 4 * mw
    o_glu = o_g + 2 * heads
    o_gate = o_glu + 2 * cw
    w_qkvo = w_in[:, :o_g].astype(BF16)
    w_gt = jnp.zeros((d, LANES), F32).at[:, :2 * heads].set(w_in[:, o_g:o_glu])
    b_gt = jnp.zeros((1, LANES), F32).at[:, :2 * heads].set(b_in[o_g:o_glu])
    w_gt_hi, w_gt_lo = _split_bf16(w_gt)
    w_ga = w_in[:, o_glu:o_glu + cw].astype(BF16)
    w_gb = w_in[:, o_glu + cw:o_gate].astype(BF16)
    w_gab = w_in[:, o_gate:].astype(BF16)

    tm = _pick(both, 512)
    qkvo = _norm_mm(xa, xb, row(g_mix), w_qkvo, row(b_in[:o_g]), act=None, out_dtype=BF16, tm=tm, tn=_pick(o_g, 2048))
    gab = _norm_mm(xa, xb, row(g_mix), w_gab, row(b_in[o_gate:]), act="sigmoid", out_dtype=BF16, tm=tm,
                   tn=_pick(2 * d, 2048))
    gates, u = _gates_glu(xa, xb, row(g_mix), w_gt_hi, w_gt_lo, b_gt, w_ga, row(b_in[o_glu:o_glu + cw]),
                          w_gb, row(b_in[o_glu + cw:o_gate]), tm=tm, tn=_pick(cw, 512))

    hs, bo, new_state = [], [], []
    pw_bf = pw_w.astype(BF16)
    for (row0, batch, seq, zero_state), (c0, n0, m0, buf) in zip(seqs, state):
        chunk = seq
        for cand in (256, 128, 64):
            if seq % CHUNK == 0 and seq % cand == 0:
                chunk = cand
                break
        gates_r = gates[row0:row0 + batch * seq, :2 * heads].reshape(-1, chunk, 2 * heads).transpose(0, 2, 1)
        hs_i, c1, n1, m1 = _mlstm(qkvo, gates, gates_r, c0, n0, m0, row0=row0, batch=batch, seq=seq,
                                  chunk=chunk, heads=heads, dh=dh)
        if zero_state:
            halo_src = u
        else:
            halo_src = jnp.pad(buf, ((0, 0), (HALO - (taps - 1), 0), (0, 0))).reshape(batch * HALO, cw)
        bo_i = _conv(u, halo_src, dw_w, row(dw_b), row(ln_g), row(ln_b), pw_bf, row(pw_b),
                     row0=row0, batch=batch, seq=seq, tt=_pick(seq, 256), zero_first=zero_state)
        hs.append(hs_i)
        bo.append(bo_i)
        u_seq = u[row0:row0 + batch * seq].reshape(batch, seq, cw)
        if seq >= taps - 1:
            buf1 = u_seq[:, seq - (taps - 1):]
        else:
            buf1 = jnp.concatenate([buf, u_seq], axis=1)[:, -(taps - 1):]
        new_state.append((c1, n1, m1, buf1))

    r_pad = jnp.zeros((d, LANES), F32).at[:, :n_exp].set(r_w)
    rb_pad = jnp.zeros((1, LANES), F32).at[:, :n_exp].set(r_b)
    r_hi, r_lo = _split_bf16(r_pad)
    tmd = _pick(both, 256)
    x1, xn2, logits = _merge(xa, xb, hs[0], hs[1], bo[0], bo[1], gab, w_a.astype(BF16), w_o.astype(BF16), row(g_ffn), r_hi, r_lo, rb_pad,
                             n_exp=n_exp, tm=tmd)
    topw, tope, rank, counts = _router(logits, tm=_pick(n, 512))

    tme = _pick(n * TOP_K, 512)
    cnt = counts[0, :n_exp].astype(I32)
    tiles_per = (cnt + tme - 1) // tme
    tile_end = jnp.cumsum(tiles_per)
    offs = (tile_end - tiles_per) * tme
    n_tiles = (n * TOP_K) // tme + n_exp
    tile_ids = jnp.arange(n_tiles, dtype=I32)
    tile_exp = jnp.minimum(jnp.sum(tile_end[None, :] <= tile_ids[:, None], axis=1), n_exp - 1)
    tile_exp = jnp.concatenate([tile_exp.astype(I32), tile_end[-1:].astype(I32)])
    pads = jnp.concatenate([offs + cnt, tiles_per * tme - cnt, tile_end[-1:]]).astype(I32)
    exp_ids = jnp.arange(n_exp, dtype=I32)
    slots = rank[:, :TOP_K] + jnp.sum(jnp.where(tope[:, :TOP_K, None] == exp_ids, offs, 0), axis=-1)

    xs = _dispatch(pads, slots.astype(I32), xn2, n_tiles, tme, n_exp=n_exp, tm=tmd)
    hid = _expert_up(tile_exp, xs, wg, bg[:, None, :], wu, bu[:, None, :], tm=tme, tn=_pick(de, 512))
    tnd = _pick(d, 1024)
    eo = _expert_down(tile_exp, hid, wd, bd[:, None, :], tm=tme, tn=tnd)
    w_pg_bf, w_ple_bf = w_pg.astype(BF16), w_ple.astype(BF16)
    ys = [_combine(slots.astype(I32), x1, topw, eo, p, row(g_ple), w_pg_bf, w_ple_bf, row(g_final),
                   row0=r0, tm=tmd, groups=d // tnd)
          for r0, p in ((0, pa), (xa.shape[0], pb))]
    return ys, new_state


def kernel(x_prompt, x_sample, state_mlstm_C, state_mlstm_n, state_mlstm_m, cache_conv, p_prompt, p_sample, norm_mix, w_in, b_in, w_a, conv_dw_w, conv_dw_b, conv_ln_g, conv_ln_b, conv_pw_w, conv_pw_b, w_o, norm_ffn, router_w, router_b, exp_w_gate, exp_b_gate, exp_w_up, exp_b_up, exp_w_down, exp_b_down, norm_ple, w_ple, w_ple_gate, norm_final):
    depth = w_in.shape[0]
    assert depth == 1, "the final RMSNorm is fused into the single layer's last kernel"
    bp, sp, d = x_prompt.shape
    bs, ss, _ = x_sample.shape
    heads, dh = state_mlstm_C.shape[2], state_mlstm_C.shape[3]
    cw = cache_conv.shape[3]
    taps = conv_dw_w.shape[1]
    n_p, n_s = bp * sp, bs * ss
    l = 0
    zero_state = (jnp.zeros((bp, heads, dh, dh), F32), jnp.zeros((bp, heads, dh), F32),
                  jnp.zeros((bp, heads), F32), jnp.zeros((bp, taps - 1, cw), F32))
    carried = (state_mlstm_C[l].astype(F32), state_mlstm_n[l].astype(F32), state_mlstm_m[l].astype(F32), cache_conv[l])
    params = (norm_mix[l], w_in[l], b_in[l], w_a[l], conv_dw_w[l], conv_dw_b[l], conv_ln_g[l], conv_ln_b[l],
              conv_pw_w[l], conv_pw_b[l], w_o[l], norm_ffn[l], router_w[l], router_b[l],
              exp_w_gate[l], exp_b_gate[l], exp_w_up[l], exp_b_up[l], exp_w_down[l], exp_b_down[l],
              norm_ple[l], w_ple[l], w_ple_gate[l], norm_final)
    seqs = [(0, bp, sp, True), (n_p, bs, ss, False)]
    (y_p, y_s), ((c_p, n_pp, m_p, b_p), (c_s, n_sp, m_s, b_s)) = _layer(
        x_prompt.reshape(n_p, d), x_sample.reshape(n_s, d), p_prompt[l].reshape(n_p, -1), p_sample[l].reshape(n_s, -1),
        seqs, [zero_state, carried], params)
    st = lambda a: a[None]
    return (y_p.reshape(bp, sp, d), y_s.reshape(bs, ss, d), st(c_p), st(n_pp), st(m_p), st(b_p),
            st(c_s), st(n_sp), st(m_s), st(b_s))
```

```python
import functools
import math

import jax
import jax.numpy as jnp
from jax import lax
from jax.experimental import pallas as pl
from jax.experimental.pallas import tpu as pltpu

F32 = jnp.float32
BF16 = jnp.bfloat16
I32 = jnp.int32
U32 = jnp.uint32

RMS_EPS = 1e-6
LN_EPS = 1e-5
TOP_K = 4
SWIGLU_ALPHA = 1.702
SWIGLU_LIMIT = 7.0
CHUNK = 64
LANES = 128
SUBLANES = 8
HALO = 32
VMEM_LIMIT = 56 * 1024 * 1024


def _cparams(*sem):
    return pltpu.CompilerParams(dimension_semantics=sem, vmem_limit_bytes=VMEM_LIMIT)


def _rms(x, g):
    ms = jnp.mean(x * x, axis=-1, keepdims=True)
    return x * lax.rsqrt(ms + RMS_EPS) * g


def _split_bf16(x):
    hi = x.astype(BF16)
    lo = (x - hi.astype(F32)).astype(BF16)
    return hi, lo


def _pack_rows(x, groups):
    w = x.shape[1] // groups
    out = []
    for g in range(groups):
        lo = lax.bitcast_convert_type(x[:, g * w:g * w + w // 2].astype(BF16).astype(F32), U32)
        hi = lax.bitcast_convert_type(x[:, g * w + w // 2:(g + 1) * w].astype(BF16).astype(F32), U32)
        out.append((lo >> 16) | hi)
    return out[0] if groups == 1 else jnp.concatenate(out, axis=1)


def _unpack_rows(p, groups):
    w = p.shape[1] // groups
    out = []
    for g in range(groups):
        pg = p[:, g * w:(g + 1) * w]
        out.append(lax.bitcast_convert_type(pg << 16, F32))
        out.append(lax.bitcast_convert_type(pg & jnp.uint32(0xFFFF0000), F32))
    return jnp.concatenate(out, axis=1)


def _store_tile_rows(ref, idx, packed):
    r = packed.shape[0]
    for q in range(SUBLANES):
        ref[idx + (pl.ds(q, r, stride=SUBLANES), slice(None))] = packed[:, q * LANES:(q + 1) * LANES]


def _load_tile_rows(ref, idx, r):
    return jnp.concatenate([ref[idx + (pl.ds(q, r, stride=SUBLANES), slice(None))] for q in range(SUBLANES)], axis=1)


def _two_part(tm, d, n_a_tiles, col=0):
    return [pl.BlockSpec((tm, d), lambda i: (jnp.minimum(i, n_a_tiles - 1), col)),
            pl.BlockSpec((tm, d), lambda i: (jnp.maximum(i - n_a_tiles, 0), col), pipeline_mode=pl.Buffered(1))]


def _rows(i, n_a_tiles, a_ref, b_ref):
    return jnp.where(i < n_a_tiles, a_ref[...], b_ref[...])


def _norm_mm_kernel(x_ref, g_ref, w_ref, b_ref, o_ref, xn_ref, *, act):
    @pl.when(pl.program_id(1) == 0)
    def _():
        xn_ref[...] = _rms(x_ref[...], g_ref[...]).astype(BF16)

    acc = jnp.dot(xn_ref[...], w_ref[...], preferred_element_type=F32) + b_ref[...]
    if act == "sigmoid":
        acc = jax.nn.sigmoid(acc)
    o_ref[...] = acc.astype(o_ref.dtype)


def _norm_mm(x, g, w, b, *, act, out_dtype, tm, tn):
    n, d = x.shape
    cols = w.shape[1]
    return pl.pallas_call(
        functools.partial(_norm_mm_kernel, act=act),
        grid=(n // tm, cols // tn),
        in_specs=[pl.BlockSpec((tm, d), lambda i, j: (i, 0)),
                  pl.BlockSpec((1, d), lambda i, j: (0, 0)),
                  pl.BlockSpec((d, tn), lambda i, j: (0, j)),
                  pl.BlockSpec((1, tn), lambda i, j: (0, j))],
        out_specs=pl.BlockSpec((tm, tn), lambda i, j: (i, j)),
        out_shape=jax.ShapeDtypeStruct((n, cols), out_dtype),
        scratch_shapes=[pltpu.VMEM((tm, d), BF16)],
        compiler_params=_cparams("parallel", "arbitrary"),
        name="norm_mm_" + str(act),
    )(x, g, w, b)


def _gates_glu_kernel(x_ref, g_ref, wgh_ref, wgl_ref, bg_ref, wa_ref, ba_ref, wb_ref, bb_ref,
                      gate_ref, u_ref, xh_ref):
    @pl.when(pl.program_id(1) == 0)
    def _():
        xn = _rms(x_ref[...], g_ref[...])
        hi, lo = _split_bf16(xn)
        xh_ref[...] = hi
        gate_ref[...] = (jnp.dot(hi, wgh_ref[...], preferred_element_type=F32)
                         + jnp.dot(hi, wgl_ref[...], preferred_element_type=F32)
                         + jnp.dot(lo, wgh_ref[...], preferred_element_type=F32)
                         + bg_ref[...])

    xh = xh_ref[...]
    a = jnp.dot(xh, wa_ref[...], preferred_element_type=F32) + ba_ref[...]
    b = jnp.dot(xh, wb_ref[...], preferred_element_type=F32) + bb_ref[...]
    u_ref[...] = a * jax.nn.sigmoid(b)


def _gates_glu(x, g, wgh, wgl, bg, wa, ba, wb, bb, *, tm, tn):
    n, d = x.shape
    cw = wa.shape[1]
    return pl.pallas_call(
        _gates_glu_kernel,
        grid=(n // tm, cw // tn),
        in_specs=[
            pl.BlockSpec((tm, d), lambda i, j: (i, 0)),
            pl.BlockSpec((1, d), lambda i, j: (0, 0)),
            pl.BlockSpec((d, LANES), lambda i, j: (0, 0)),
            pl.BlockSpec((d, LANES), lambda i, j: (0, 0)),
            pl.BlockSpec((1, LANES), lambda i, j: (0, 0)),
            pl.BlockSpec((d, tn), lambda i, j: (0, j)),
            pl.BlockSpec((1, tn), lambda i, j: (0, j)),
            pl.BlockSpec((d, tn), lambda i, j: (0, j)),
            pl.BlockSpec((1, tn), lambda i, j: (0, j))],
        out_specs=[pl.BlockSpec((tm, LANES), lambda i, j: (i, 0)),
                   pl.BlockSpec((tm, tn), lambda i, j: (i, j))],
        out_shape=[jax.ShapeDtypeStruct((n, LANES), F32),
                   jax.ShapeDtypeStruct((n, cw), F32)],
        scratch_shapes=[pltpu.VMEM((tm, d), BF16)],
        compiler_params=_cparams("parallel", "arbitrary"),
        name="gates_glu",
    )(x, g, wgh, wgl, bg, wa, ba, wb, bb)


def _mlstm_kernel(q_ref, k_ref, v_ref, o_ref, gc_ref, gr_ref, c0_ref, n0_ref, m0_ref,
                  hs_ref, c1_ref, n1_ref, m1_ref, c_sc, n_sc, m_sc, *, heads, dh, scale):
    c = pl.program_id(1)
    nc = pl.num_programs(1)
    L = q_ref.shape[0]

    @pl.when(c == 0)
    def _():
        c_sc[...] = c0_ref[0]
        n_sc[...] = n0_ref[0]
        m_sc[...] = m0_ref[0]

    row = lax.broadcasted_iota(I32, (L, L), 0)
    col = lax.broadcasted_iota(I32, (L, L), 1)
    causal = col <= row
    gc = gc_ref[...]
    gr = gr_ref[0]

    for h in range(heads):
        sl = slice(h * dh, (h + 1) * dh)
        q = q_ref[:, sl]
        ks = k_ref[:, sl] * scale
        v = v_ref[:, sl]
        ig_col = gc[:, h:h + 1]
        lf_col = jax.nn.log_sigmoid(gc[:, heads + h:heads + h + 1])
        ig_row = gr[h:h + 1, :]
        lf_row = jax.nn.log_sigmoid(gr[heads + h:heads + h + 1, :])
        f_col = jnp.sum(jnp.where(causal, lf_row, 0.0), axis=1, keepdims=True)
        f_row = jnp.sum(jnp.where(row <= col, lf_col, 0.0), axis=0, keepdims=True)
        m0 = m_sc[h]
        n0 = n_sc[h]
        c0 = c_sc[h]
        dmat = jnp.where(causal, f_col - f_row + ig_row, -jnp.inf)
        inter = f_col + m0
        m_col = jnp.maximum(inter, jnp.max(dmat, axis=1, keepdims=True))
        a_col = jnp.exp(inter - m_col)
        s = lax.dot_general(q, ks, (((1,), (1,)), ((), ())), preferred_element_type=F32)
        w = jnp.exp(dmat - m_col) * s
        qc = lax.dot_general(q, c0.astype(BF16), (((1,), (1,)), ((), ())), preferred_element_type=F32)
        num = a_col * qc + jnp.dot(w.astype(BF16), v, preferred_element_type=F32)
        qf = q.astype(F32)
        den_raw = a_col * jnp.sum(qf * n0, axis=1, keepdims=True) + jnp.sum(w, axis=1, keepdims=True)
        den = jnp.maximum(jnp.abs(den_raw), jnp.exp(-m_col))
        hval = num / den
        hs_ref[:, sl] = (jax.nn.sigmoid(o_ref[:, sl].astype(F32)) * hval).astype(hs_ref.dtype)
        m_last = m_col[L - 1:L, :]
        f_last = f_col[L - 1:L, :]
        w_end = jnp.exp(f_last - f_col + ig_col - m_last)
        decay = jnp.exp(inter[L - 1:L, :] - m_last)
        vw = (v.astype(F32) * w_end).astype(BF16)
        c_sc[h] = decay * c0 + lax.dot_general(vw, ks, (((0,), (0,)), ((), ())), preferred_element_type=F32)
        n_sc[h] = decay * n0 + jnp.sum(ks.astype(F32) * w_end, axis=0, keepdims=True)
        m_sc[h] = m_last

    @pl.when(c == nc - 1)
    def _():
        c1_ref[0] = c_sc[...]
        n1_ref[0] = n_sc[...]
        m1_ref[0] = m_sc[...]


def _mlstm(qkvo, gates_c, gates_r, c0, n0, m0, *, row0, batch, seq, chunk, heads, dh):
    width = heads * dh
    nc = seq // chunk
    rb0 = row0 // chunk
    n0r = n0.reshape(batch, heads, 1, dh)
    m0r = m0.reshape(batch, heads, 1, 1)

    def blk(part):
        return pl.BlockSpec((chunk, width), lambda b, c: (rb0 + b * nc + c, part))

    state_spec = lambda shp: pl.BlockSpec((1,) + shp, lambda b, c: (b, 0, 0, 0))
    hs, c1, n1, m1 = pl.pallas_call(
        functools.partial(_mlstm_kernel, heads=heads, dh=dh, scale=dh ** -0.5),
        grid=(batch, nc),
        in_specs=[blk(0), blk(1), blk(2), blk(3),
                  pl.BlockSpec((chunk, LANES), lambda b, c: (rb0 + b * nc + c, 0)),
                  pl.BlockSpec((1, 2 * heads, chunk), lambda b, c: (b * nc + c, 0, 0)),
                  state_spec((heads, dh, dh)), state_spec((heads, 1, dh)), state_spec((heads, 1, 1))],
        out_specs=[pl.BlockSpec((chunk, width), lambda b, c: (b * nc + c, 0)),
                   state_spec((heads, dh, dh)), state_spec((heads, 1, dh)), state_spec((heads, 1, 1))],
        out_shape=[jax.ShapeDtypeStruct((batch * seq, width), BF16),
                   jax.ShapeDtypeStruct((batch, heads, dh, dh), F32),
                   jax.ShapeDtypeStruct((batch, heads, 1, dh), F32),
                   jax.ShapeDtypeStruct((batch, heads, 1, 1), F32)],
        scratch_shapes=[pltpu.VMEM((heads, dh, dh), F32), pltpu.VMEM((heads, 1, dh), F32),
                        pltpu.VMEM((heads, 1, 1), F32)],
        compiler_params=_cparams("parallel", "arbitrary"),
        name="mlstm",
    )(qkvo, qkvo, qkvo, qkvo, gates_c, gates_r, c0, n0r, m0r)
    return hs, c1, n1.reshape(batch, heads, dh), m1.reshape(batch, heads)


def _conv_kernel(halo_ref, u_ref, dw_ref, dwb_ref, lg_ref, lb_ref, pw_ref, pwb_ref, o_ref, full_sc, sh_sc, wb_sc, y_sc,
                 *, taps, zero_first, rb, cb):
    tt, cw = u_ref.shape
    halo = halo_ref[...]
    if zero_first:
        halo = jnp.where(pl.program_id(1) == 0, 0.0, halo)
    full_sc[0:HALO, :] = halo
    full_sc[HALO:HALO + tt, :] = u_ref[...]
    span = tt + HALO - SUBLANES
    for s in range(1, SUBLANES):
        sh_sc[s - 1] = full_sc[s:s + span, :]
    for j in range(taps):
        wb_sc[j] = jnp.broadcast_to(dw_ref[j:j + 1, :], (SUBLANES, cw))
    base = HALO - (taps - 1)
    n_acc = 1

    for c0 in range(0, cw, cb):
        cols = slice(c0, c0 + cb)

        def body(r, carry):
            r0 = pl.multiple_of(r * rb, rb)
            accs = [jnp.zeros((rb, cb), F32) + dwb_ref[:, cols]] + [None] * (n_acc - 1)
            for j in range(taps):
                a, s = divmod(base + j, SUBLANES)
                rows = pl.ds(r0 + a * SUBLANES, rb)
                src = full_sc[rows, cols] if s == 0 else sh_sc[s - 1, rows, cols]
                term = jnp.concatenate([wb_sc[j, :, cols]] * (rb // SUBLANES), axis=0) * src
                accs[j % n_acc] = term if accs[j % n_acc] is None else accs[j % n_acc] + term
            y_sc[pl.ds(r0, rb), cols] = functools.reduce(lambda p, q: p + q, accs)
            return carry

        lax.fori_loop(0, tt // rb, body, 0)
    y = y_sc[...]
    mu = jnp.mean(y, axis=-1, keepdims=True)
    yc = y - mu
    yn = yc * lax.rsqrt(jnp.mean(yc * yc, axis=-1, keepdims=True) + LN_EPS) * lg_ref[...] + lb_ref[...]
    act = yn * jax.nn.sigmoid(yn)
    o_ref[...] = (jnp.dot(act.astype(BF16), pw_ref[...], preferred_element_type=F32) + pwb_ref[...]).astype(o_ref.dtype)


def _conv(u, halo_src, dw_w, dw_b, ln_g, ln_b, pw_w, pw_b, *, row0, batch, seq, tt, zero_first):
    cw = u.shape[1]
    d = pw_w.shape[1]
    taps = dw_w.shape[0]
    nt = seq // tt
    tb0 = row0 // tt
    if zero_first:
        hb0, per = row0 // HALO, tt // HALO
        halo_map = lambda b, t: (jnp.maximum(hb0 + (b * nt + t) * per - 1, 0), 0)
    else:
        assert nt == 1, "a carried conv cache is only supported for single-tile sequences"
        halo_map = lambda b, t: (b, 0)
    dw_pad = jnp.zeros((HALO, cw), F32).at[:taps].set(dw_w)
    return pl.pallas_call(
        functools.partial(_conv_kernel, taps=taps, zero_first=zero_first, rb=min(4 * SUBLANES, tt), cb=min(2 * LANES, cw)),
        grid=(batch, nt),
        in_specs=[pl.BlockSpec((HALO, cw), halo_map),
                  pl.BlockSpec((tt, cw), lambda b, t: (tb0 + b * nt + t, 0)),
                  pl.BlockSpec((HALO, cw), lambda b, t: (0, 0)),
                  pl.BlockSpec((1, cw), lambda b, t: (0, 0)),
                  pl.BlockSpec((1, cw), lambda b, t: (0, 0)),
                  pl.BlockSpec((1, cw), lambda b, t: (0, 0)),
                  pl.BlockSpec((cw, d), lambda b, t: (0, 0)),
                  pl.BlockSpec((1, d), lambda b, t: (0, 0))],
        out_specs=pl.BlockSpec((tt, d), lambda b, t: (b * nt + t, 0)),
        out_shape=jax.ShapeDtypeStruct((batch * seq, d), BF16),
        scratch_shapes=[pltpu.VMEM((HALO + tt, cw), F32),
                        pltpu.VMEM((SUBLANES - 1, HALO + tt - SUBLANES, cw), F32),
                        pltpu.VMEM((taps, SUBLANES, cw), F32),
                        pltpu.VMEM((tt, cw), F32)],
        compiler_params=_cparams("parallel", "arbitrary"),
        name="conv_module",
    )(halo_src, u, dw_pad, dw_b, ln_g, ln_b, pw_w, pw_b)


def _merge_kernel(xa_ref, xb_ref, hsa_ref, hsb_ref, boa_ref, bob_ref, gaa_ref, gab_ref, gba_ref, gbb_ref,
                  wa_ref, wo_ref, g_ref, rwh_ref, rwl_ref, rb_ref, x1_ref, xn_ref, lg_ref, *, n_exp, n_a_tiles):
    i = pl.program_id(0)
    a = jnp.dot(_rows(i, n_a_tiles, hsa_ref, hsb_ref), wa_ref[...], preferred_element_type=F32)
    bo = _rows(i, n_a_tiles, boa_ref, bob_ref)
    ga = _rows(i, n_a_tiles, gaa_ref, gab_ref)
    gb = _rows(i, n_a_tiles, gba_ref, gbb_ref)
    mix = ga.astype(F32) * a + gb.astype(F32) * bo.astype(F32)
    x1 = _rows(i, n_a_tiles, xa_ref, xb_ref) + jnp.dot(mix.astype(BF16), wo_ref[...], preferred_element_type=F32)
    x1_ref[...] = x1
    xn = _rms(x1, g_ref[...])
    _store_tile_rows(xn_ref, (), _pack_rows(xn, 1))
    hi, lo = _split_bf16(xn)
    logits = (jnp.dot(hi, rwh_ref[...], preferred_element_type=F32)
              + jnp.dot(hi, rwl_ref[...], preferred_element_type=F32)
              + jnp.dot(lo, rwh_ref[...], preferred_element_type=F32)
              + rb_ref[...])
    lane = lax.broadcasted_iota(I32, logits.shape, 1)
    lg_ref[...] = jnp.where(lane < n_exp, logits, -jnp.inf)


def _merge(xa, xb, hsa, hsb, boa, bob, gaba, gabb, w_a, w_o, g, rwh, rwl, rb, *, n_exp, tm):
    d = xa.shape[1]
    n = xa.shape[0] + xb.shape[0]
    n_a_tiles = xa.shape[0] // tm
    mw = hsa.shape[1]
    const = lambda shp: pl.BlockSpec(shp, lambda i: (0, 0), pipeline_mode=pl.Buffered(1))
    return pl.pallas_call(
        functools.partial(_merge_kernel, n_exp=n_exp, n_a_tiles=n_a_tiles),
        grid=(n // tm,),
        in_specs=_two_part(tm, d, n_a_tiles) + _two_part(tm, mw, n_a_tiles) + _two_part(tm, d, n_a_tiles)
        + _two_part(tm, d, n_a_tiles, 0) + _two_part(tm, d, n_a_tiles, 1) + [
            const((mw, d)), const((d, d)), const((1, d)),
            const((d, LANES)), const((d, LANES)), const((1, LANES))],
        out_specs=[pl.BlockSpec((tm, d), lambda i: (i, 0)),
                   pl.BlockSpec((tm * SUBLANES, LANES), lambda i: (i, 0)),
                   pl.BlockSpec((tm, LANES), lambda i: (i, 0))],
        out_shape=[jax.ShapeDtypeStruct((n, d), F32),
                   jax.ShapeDtypeStruct((n * SUBLANES, LANES), U32),
                   jax.ShapeDtypeStruct((n, LANES), F32)],
        compiler_params=_cparams("parallel"),
        name="merge",
    )(xa, xb, hsa, hsb, boa, bob, gaba, gabb, gaba, gabb, w_a, w_o, g, rwh, rwl, rb)


def _router_kernel(lg_ref, w_ref, e_ref, r_ref, cnt_ref, carry_sc):
    i = pl.program_id(0)

    @pl.when(i == 0)
    def _():
        carry_sc[...] = jnp.zeros_like(carry_sc)

    l = lg_ref[...]
    tm = l.shape[0]
    lane = lax.broadcasted_iota(I32, l.shape, 1)
    lane_f = lane.astype(F32)
    vals, hots, idxs = [], [], []
    for _ in range(TOP_K):
        mx = jnp.max(l, axis=1, keepdims=True)
        idx = jnp.min(jnp.where(l == mx, lane_f, float(LANES)), axis=1, keepdims=True)
        hot = lane_f == idx
        vals.append(mx)
        hots.append(hot)
        idxs.append(idx)
        l = jnp.where(hot, -jnp.inf, l)
    ex = [jnp.exp(v - vals[0]) for v in vals]
    tot = ex[0] + ex[1] + ex[2] + ex[3]
    sel = jnp.zeros(l.shape, F32)
    for hot in hots:
        sel = jnp.where(hot, 1.0, sel)
    rr = lax.broadcasted_iota(I32, (tm, tm), 0)
    cc = lax.broadcasted_iota(I32, (tm, tm), 1)
    tri = jnp.where(cc < rr, 1.0, 0.0).astype(BF16)
    cum = jnp.dot(tri, sel.astype(BF16), preferred_element_type=F32) + carry_sc[...]
    w_out = jnp.zeros(l.shape, F32)
    e_out = jnp.zeros(l.shape, I32)
    r_out = jnp.zeros(l.shape, I32)
    for k in range(TOP_K):
        rank = jnp.sum(jnp.where(hots[k], cum, 0.0), axis=1, keepdims=True).astype(I32)
        w_out = jnp.where(lane == k, ex[k] / tot, w_out)
        e_out = jnp.where(lane == k, idxs[k].astype(I32), e_out)
        r_out = jnp.where(lane == k, rank, r_out)
    w_ref[...] = w_out
    e_ref[...] = e_out
    r_ref[...] = r_out
    carry_sc[...] = carry_sc[...] + jnp.sum(sel, axis=0, keepdims=True)
    cnt_ref[...] = carry_sc[...]


def _router(logits, *, tm):
    n = logits.shape[0]
    blk = pl.BlockSpec((tm, LANES), lambda i: (i, 0))
    return pl.pallas_call(
        _router_kernel,
        grid=(n // tm,),
        in_specs=[blk],
        out_specs=[blk, blk, blk, pl.BlockSpec((1, LANES), lambda i: (0, 0))],
        out_shape=[jax.ShapeDtypeStruct((n, LANES), F32), jax.ShapeDtypeStruct((n, LANES), I32),
                   jax.ShapeDtypeStruct((n, LANES), I32), jax.ShapeDtypeStruct((1, LANES), F32)],
        scratch_shapes=[pltpu.VMEM((1, LANES), F32)],
        compiler_params=_cparams("arbitrary"),
        name="router",
    )(logits)


def _dispatch_kernel(pad_ref, slot_ref, x_ref, xs_ref, zero_sc, sem, zsem, *, n_exp, n_tiles):
    tm = x_ref.shape[0] // SUBLANES
    tme = zero_sc.shape[0] // SUBLANES

    def tile_rows(row, size):
        return pl.ds(pl.multiple_of(row * SUBLANES, SUBLANES), size * SUBLANES)

    def zero_fill(wait):
        def run(row, size):
            cp = pltpu.make_async_copy(zero_sc.at[pl.ds(0, size * SUBLANES)], xs_ref.at[tile_rows(row, size)], zsem)
            if wait:
                cp.wait()
            else:
                cp.start()

        def per_expert(e, carry):
            first, length = pad_ref[e], pad_ref[n_exp + e]
            size = tme // 2
            while size >= 1:
                @pl.when((length & size) != 0)
                def _(size=size):
                    run(first + (length & ~(2 * size - 1)), size)
                size //= 2
            return carry

        def per_tile(t, carry):
            run(t * tme, tme)
            return carry

        lax.fori_loop(0, n_exp, per_expert, 0)
        lax.fori_loop(pad_ref[2 * n_exp], n_tiles, per_tile, 0)

    @pl.when(pl.program_id(0) == 0)
    def _():
        zero_sc[...] = jnp.zeros_like(zero_sc)
        zero_fill(wait=False)
        zero_fill(wait=True)

    def row_copy(r, k):
        return pltpu.make_async_copy(x_ref.at[tile_rows(r, 1)], xs_ref.at[tile_rows(slot_ref[0, 0, r * TOP_K + k], 1)], sem)

    def start_rows(r, carry):
        for k in range(TOP_K):
            row_copy(r, k).start(priority=k % 2)
        return carry

    def wait_rows(r, carry):
        for k in range(TOP_K):
            row_copy(r, k).wait()
        return carry

    lax.fori_loop(0, tm, start_rows, 0)
    lax.fori_loop(0, tm, wait_rows, 0)


def _dispatch(pads, slots, xn, n_tiles, tme, *, n_exp, tm):
    n = xn.shape[0] // SUBLANES
    n_slots = n_tiles * tme
    return pl.pallas_call(
        functools.partial(_dispatch_kernel, n_exp=n_exp, n_tiles=n_tiles),
        grid_spec=pltpu.PrefetchScalarGridSpec(
            num_scalar_prefetch=1,
            grid=(n // tm,),
            in_specs=[pl.BlockSpec((1, 1, tm * TOP_K), lambda i, pads: (i, 0, 0), memory_space=pltpu.SMEM),
                      pl.BlockSpec((tm * SUBLANES, LANES), lambda i, pads: (i, 0))],
            out_specs=pl.BlockSpec(memory_space=pl.ANY),
            scratch_shapes=[pltpu.VMEM((tme * SUBLANES, LANES), xn.dtype), pltpu.SemaphoreType.DMA(()),
                            pltpu.SemaphoreType.DMA(())]),
        out_shape=jax.ShapeDtypeStruct((n_slots * SUBLANES, LANES), xn.dtype),
        compiler_params=_cparams("arbitrary"),
        name="dispatch",
    )(pads, slots.reshape(n // tm, 1, tm * TOP_K), xn)


def _tile_state(te_ref, axis):
    i = pl.program_id(axis)
    used = i < te_ref[pl.num_programs(axis)]
    new_weights = jnp.logical_or(i == 0, te_ref[i] != te_ref[jnp.maximum(i - 1, 0)])
    return used, new_weights


def _up_kernel(te_ref, xs_ref, wg_ref, bg_ref, wu_ref, bu_ref, h_ref, wg_sc, wu_sc):
    used, new_weights = _tile_state(te_ref, 1)

    @pl.when(jnp.logical_and(used, new_weights))
    def _():
        wg_sc[...] = wg_ref[0].astype(BF16)
        wu_sc[...] = wu_ref[0].astype(BF16)

    @pl.when(used)
    def _():
        x = _unpack_rows(_load_tile_rows(xs_ref, (), h_ref.shape[0]), 1).astype(BF16)
        g = jnp.minimum(jnp.dot(x, wg_sc[...], preferred_element_type=F32) + bg_ref[0], SWIGLU_LIMIT)
        u = jnp.clip(jnp.dot(x, wu_sc[...], preferred_element_type=F32) + bu_ref[0], -SWIGLU_LIMIT, SWIGLU_LIMIT)
        h_ref[...] = (g * jax.nn.sigmoid(SWIGLU_ALPHA * g) * (u + 1.0)).astype(h_ref.dtype)

    @pl.when(jnp.logical_not(used))
    def _():
        h_ref[...] = jnp.zeros_like(h_ref)


def _down_kernel(te_ref, h_ref, wd_ref, bd_ref, o_ref, wd_sc):
    used, new_weights = _tile_state(te_ref, 0)

    @pl.when(jnp.logical_and(used, new_weights))
    def _():
        wd_sc[...] = wd_ref[0].astype(BF16)

    @pl.when(used)
    def _():
        out = jnp.dot(h_ref[...], wd_sc[...], preferred_element_type=F32) + bd_ref[0]
        _store_tile_rows(o_ref, (), _pack_rows(out, 1))

    @pl.when(jnp.logical_not(used))
    def _():
        o_ref[...] = jnp.zeros_like(o_ref)


def _used_tile(i, te):
    return jnp.where(i < te[te.shape[0] - 1], i, 0)


def _expert_up(tile_exp, xs, wg, bg, wu, bu, *, tm, tn):
    s = xs.shape[0] // SUBLANES
    d, de = wg.shape[1], wg.shape[2]
    wspec = pl.BlockSpec((1, d, tn), lambda j, i, te: (te[i], 0, j))
    bspec = pl.BlockSpec((1, 1, tn), lambda j, i, te: (te[i], 0, j))
    return pl.pallas_call(
        _up_kernel,
        grid_spec=pltpu.PrefetchScalarGridSpec(
            num_scalar_prefetch=1,
            grid=(de // tn, s // tm),
            in_specs=[pl.BlockSpec((tm * SUBLANES, LANES), lambda j, i, te: (_used_tile(i, te), 0)),
                      wspec, bspec, wspec, bspec],
            out_specs=pl.BlockSpec((tm, tn), lambda j, i, te: (i, j)),
            scratch_shapes=[pltpu.VMEM((d, tn), BF16), pltpu.VMEM((d, tn), BF16)]),
        out_shape=jax.ShapeDtypeStruct((s, de), BF16),
        compiler_params=_cparams("parallel", "arbitrary"),
        name="expert_up",
    )(tile_exp, xs, wg, bg, wu, bu)


def _expert_down(tile_exp, hid, wd, bd, *, tm):
    s, de = hid.shape
    d = wd.shape[2]
    return pl.pallas_call(
        _down_kernel,
        grid_spec=pltpu.PrefetchScalarGridSpec(
            num_scalar_prefetch=1,
            grid=(s // tm,),
            in_specs=[pl.BlockSpec((tm, de), lambda i, te: (_used_tile(i, te), 0)),
                      pl.BlockSpec((1, de, d), lambda i, te: (te[i], 0, 0), pipeline_mode=pl.Buffered(1)),
                      pl.BlockSpec((1, 1, d), lambda i, te: (te[i], 0, 0))],
            out_specs=pl.BlockSpec((tm * SUBLANES, LANES), lambda i, te: (i, 0)),
            scratch_shapes=[pltpu.VMEM((de, d), BF16)]),
        out_shape=jax.ShapeDtypeStruct((s * SUBLANES, LANES), U32),
        compiler_params=_cparams("arbitrary"),
        name="expert_down",
    )(tile_exp, hid, wd, bd)


def _combine_kernel(slot_ref, nslot_ref, x1_ref, tw_ref, eo_ref, p_ref, gp_ref, wpg_ref, wple_ref, gf_ref, y_ref,
                    rows_sc, sem, *, nt):
    tm = x1_ref.shape[0]
    i = pl.program_id(0)

    def tile_rows(row):
        return pl.ds(pl.multiple_of(row * SUBLANES, SUBLANES), SUBLANES)

    def row_copy(ref, buf, r, k):
        return pltpu.make_async_copy(eo_ref.at[tile_rows(ref[0, 0, r * TOP_K + k])],
                                     rows_sc.at[buf, k, tile_rows(r)], sem.at[buf])

    def start_tile(ref, buf):
        def body(r, carry):
            for k in range(TOP_K):
                row_copy(ref, buf, r, k).start(priority=k % 2)
            return carry
        lax.fori_loop(0, tm, body, 0)

    @pl.when(i == 0)
    def _():
        start_tile(slot_ref, 0)

    @pl.when(i + 1 < nt)
    def _():
        start_tile(nslot_ref, (i + 1) % 2)

    buf = i % 2

    def wait_body(r, carry):
        for k in range(TOP_K):
            row_copy(slot_ref, buf, r, k).wait()
        return carry

    lax.fori_loop(0, tm, wait_body, 0)
    tw = tw_ref[...]
    moe = jnp.zeros(x1_ref.shape, F32)
    for k in range(TOP_K):
        moe = moe + tw[:, k:k + 1] * _unpack_rows(_load_tile_rows(rows_sc, (buf, k), tm), 1)
    x2 = x1_ref[...] + moe
    gate = jax.nn.sigmoid(jnp.dot(_rms(x2, gp_ref[...]).astype(BF16), wpg_ref[...], preferred_element_type=F32))
    ple = jnp.dot(p_ref[...].astype(BF16), wple_ref[...], preferred_element_type=F32)
    x3 = x2 + ple * gate
    y_ref[...] = _rms(x3, gf_ref[...])


def _combine(slots, x1, topw, eo, p, g_ple, w_pg, w_ple, g_final, *, row0, tm):
    n, d = x1.shape
    n_part, pd = p.shape
    nt = n_part // tm
    t0 = row0 // tm
    const = lambda shp: pl.BlockSpec(shp, lambda i: (0, 0), pipeline_mode=pl.Buffered(1))
    slots3 = slots.reshape(n // tm, 1, tm * TOP_K)
    return pl.pallas_call(
        functools.partial(_combine_kernel, nt=nt),
        grid=(nt,),
        in_specs=[pl.BlockSpec((1, 1, tm * TOP_K), lambda i: (t0 + i, 0, 0), memory_space=pltpu.SMEM),
                  pl.BlockSpec((1, 1, tm * TOP_K), lambda i: (t0 + jnp.minimum(i + 1, nt - 1), 0, 0),
                               memory_space=pltpu.SMEM),
                  pl.BlockSpec((tm, d), lambda i: (t0 + i, 0)),
                  pl.BlockSpec((tm, LANES), lambda i: (t0 + i, 0)),
                  pl.BlockSpec(memory_space=pl.ANY),
                  pl.BlockSpec((tm, pd), lambda i: (i, 0)),
                  const((1, d)), const((d, d)), const((pd, d)), const((1, d))],
        out_specs=pl.BlockSpec((tm, d), lambda i: (i, 0)),
        out_shape=jax.ShapeDtypeStruct((n_part, d), F32),
        scratch_shapes=[pltpu.VMEM((2, TOP_K, tm * SUBLANES, LANES), eo.dtype), pltpu.SemaphoreType.DMA((2,))],
        compiler_params=_cparams("arbitrary"),
        name="combine",
    )(slots3, slots3, x1, topw, eo, p, g_ple, w_pg, w_ple, g_final)


def _pick(n, pref):
    t = pref
    while n % t:
        t //= 2
    return t


def _layer(xa, xb, pa, pb, seqs, state, params):
    (g_mix, w_in, b_in, w_a, dw_w, dw_b, ln_g, ln_b, pw_w, pw_b, w_o, g_ffn,
     r_w, r_b, wg, bg, wu, bu, wd, bd, g_ple, w_ple, w_pg, g_final) = params
    d = xa.shape[1]
    n = xa.shape[0] + xb.shape[0]
    both = math.gcd(xa.shape[0], xb.shape[0])
    heads = state[0][1].shape[1]
    dh = state[0][1].shape[2]
    mw = heads * dh
    cw = dw_w.shape[1]
    taps = dw_w.shape[0]
    n_exp = r_w.shape[1]
    de = wg.shape[2]
    row = lambda v: v.reshape(1, -1)

    o_g = 4 * mw
    o_glu = o_g + 2 * heads
    o_gate = o_glu + 2 * cw
    w_qkvo = w_in[:, :o_g].astype(BF16)
    w_gt = jnp.zeros((d, LANES), F32).at[:, :2 * heads].set(w_in[:, o_g:o_glu])
    b_gt = jnp.zeros((1, LANES), F32).at[:, :2 * heads].set(b_in[o_g:o_glu])
    w_gt_hi, w_gt_lo = _split_bf16(w_gt)
    w_ga = w_in[:, o_glu:o_glu + cw].astype(BF16)
    w_gb = w_in[:, o_glu + cw:o_gate].astype(BF16)
    w_gab = w_in[:, o_gate:].astype(BF16)

    hs, bo, gab, new_state = [], [], [], []
    pw_bf = pw_w.astype(BF16)
    for x, (batch, seq, zero_state), (c0, n0, m0, buf) in zip((xa, xb), seqs, state):
        tm = _pick(x.shape[0], 1024)
        qkvo = _norm_mm(x, row(g_mix), w_qkvo, row(b_in[:o_g]), act=None, out_dtype=BF16, tm=tm, tn=_pick(o_g, 1024))
        gab_i = _norm_mm(x, row(g_mix), w_gab, row(b_in[o_gate:]), act="sigmoid", out_dtype=BF16, tm=tm,
                         tn=_pick(2 * d, 1024))
        gates, u = _gates_glu(x, row(g_mix), w_gt_hi, w_gt_lo, b_gt, w_ga, row(b_in[o_glu:o_glu + cw]),
                              w_gb, row(b_in[o_glu + cw:o_gate]), tm=_pick(x.shape[0], 512), tn=_pick(cw, 512))
        chunk = seq
        for cand in (256, 128, 64):
            if seq % CHUNK == 0 and seq % cand == 0:
                chunk = cand
                break
        gates_r = gates[:, :2 * heads].reshape(-1, chunk, 2 * heads).transpose(0, 2, 1)
        hs_i, c1, n1, m1 = _mlstm(qkvo, gates, gates_r, c0, n0, m0, row0=0, batch=batch, seq=seq,
                                  chunk=chunk, heads=heads, dh=dh)
        if zero_state:
            halo_src = u
        else:
            halo_src = jnp.pad(buf, ((0, 0), (HALO - (taps - 1), 0), (0, 0))).reshape(batch * HALO, cw)
        bo_i = _conv(u, halo_src, dw_w, row(dw_b), row(ln_g), row(ln_b), pw_bf, row(pw_b),
                     row0=0, batch=batch, seq=seq, tt=_pick(seq, 256), zero_first=zero_state)
        hs.append(hs_i)
        bo.append(bo_i)
        gab.append(gab_i)
        u_seq = u.reshape(batch, seq, cw)
        if seq >= taps - 1:
            buf1 = u_seq[:, seq - (taps - 1):]
        else:
            buf1 = jnp.concatenate([buf, u_seq], axis=1)[:, -(taps - 1):]
        new_state.append((c1, n1, m1, buf1))

    r_pad = jnp.zeros((d, LANES), F32).at[:, :n_exp].set(r_w)
    rb_pad = jnp.zeros((1, LANES), F32).at[:, :n_exp].set(r_b)
    r_hi, r_lo = _split_bf16(r_pad)
    tmd = _pick(both, 256)
    assert d // 2 == SUBLANES * LANES, "a packed token row must fill exactly one sublane tile"
    x1, xn2, logits = _merge(xa, xb, hs[0], hs[1], bo[0], bo[1], gab[0], gab[1], w_a.astype(BF16), w_o.astype(BF16),
                             row(g_ffn), r_hi, r_lo, rb_pad, n_exp=n_exp, tm=tmd)
    topw, tope, rank, counts = _router(logits, tm=_pick(n, 512))

    tme = _pick(n * TOP_K, 512)
    cnt = counts[0, :n_exp].astype(I32)
    tiles_per = (cnt + tme - 1) // tme
    tile_end = jnp.cumsum(tiles_per)
    offs = (tile_end - tiles_per) * tme
    n_tiles = (n * TOP_K) // tme + n_exp
    tile_ids = jnp.arange(n_tiles, dtype=I32)
    tile_exp = jnp.minimum(jnp.sum(tile_end[None, :] <= tile_ids[:, None], axis=1), n_exp - 1)
    tile_exp = jnp.concatenate([tile_exp.astype(I32), tile_end[-1:].astype(I32)])
    pads = jnp.concatenate([offs + cnt, tiles_per * tme - cnt, tile_end[-1:]]).astype(I32)
    exp_ids = jnp.arange(n_exp, dtype=I32)
    slots = rank[:, :TOP_K] + jnp.sum(jnp.where(tope[:, :TOP_K, None] == exp_ids, offs, 0), axis=-1)

    xs = _dispatch(pads, slots.astype(I32), xn2, n_tiles, tme, n_exp=n_exp, tm=tmd)
    hid = _expert_up(tile_exp, xs, wg, bg[:, None, :], wu, bu[:, None, :], tm=tme, tn=_pick(de, 512))
    eo = _expert_down(tile_exp, hid, wd, bd[:, None, :], tm=tme)
    w_pg_bf, w_ple_bf = w_pg.astype(BF16), w_ple.astype(BF16)
    ys = [_combine(slots.astype(I32), x1, topw, eo, p, row(g_ple), w_pg_bf, w_ple_bf, row(g_final), row0=r0, tm=tmd)
          for r0, p in ((0, pa), (xa.shape[0], pb))]
    return ys, new_state


def kernel(x_prompt, x_sample, state_mlstm_C, state_mlstm_n, state_mlstm_m, cache_conv, p_prompt, p_sample, norm_mix, w_in, b_in, w_a, conv_dw_w, conv_dw_b, conv_ln_g, conv_ln_b, conv_pw_w, conv_pw_b, w_o, norm_ffn, router_w, router_b, exp_w_gate, exp_b_gate, exp_w_up, exp_b_up, exp_w_down, exp_b_down, norm_ple, w_ple, w_ple_gate, norm_final):
    depth = w_in.shape[0]
    assert depth == 1, "the final RMSNorm is fused into the single layer's last kernel"
    bp, sp, d = x_prompt.shape
    bs, ss, _ = x_sample.shape
    heads, dh = state_mlstm_C.shape[2], state_mlstm_C.shape[3]
    cw = cache_conv.shape[3]
    taps = conv_dw_w.shape[1]
    n_p, n_s = bp * sp, bs * ss
    l = 0
    zero_state = (jnp.zeros((bp, heads, dh, dh), F32), jnp.zeros((bp, heads, dh), F32),
                  jnp.zeros((bp, heads), F32), jnp.zeros((bp, taps - 1, cw), F32))
    carried = (state_mlstm_C[l].astype(F32), state_mlstm_n[l].astype(F32), state_mlstm_m[l].astype(F32), cache_conv[l])
    params = (norm_mix[l], w_in[l], b_in[l], w_a[l], conv_dw_w[l], conv_dw_b[l], conv_ln_g[l], conv_ln_b[l],
              conv_pw_w[l], conv_pw_b[l], w_o[l], norm_ffn[l], router_w[l], router_b[l],
              exp_w_gate[l], exp_b_gate[l], exp_w_up[l], exp_b_up[l], exp_w_down[l], exp_b_down[l],
              norm_ple[l], w_ple[l], w_ple_gate[l], norm_final)
    seqs = [(bp, sp, True), (bs, ss, False)]
    (y_p, y_s), ((c_p, n_pp, m_p, b_p), (c_s, n_sp, m_s, b_s)) = _layer(
        x_prompt.reshape(n_p, d), x_sample.reshape(n_s, d), p_prompt[l].reshape(n_p, -1), p_sample[l].reshape(n_s, -1),
        seqs, [zero_state, carried], params)
    st = lambda a: a[None]
    return (y_p.reshape(bp, sp, d), y_s.reshape(bs, ss, d), st(c_p), st(n_pp), st(m_p), st(b_p),
            st(c_s), st(n_sp), st(m_s), st(b_s))
```

```python
import functools
import math

import jax
import jax.numpy as jnp
from jax import lax
from jax.experimental import pallas as pl
from jax.experimental.pallas import tpu as pltpu

F32 = jnp.float32
BF16 = jnp.bfloat16
I32 = jnp.int32
U32 = jnp.uint32

RMS_EPS = 1e-6
LN_EPS = 1e-5
TOP_K = 4
SWIGLU_ALPHA = 1.702
SWIGLU_LIMIT = 7.0
CHUNK = 64
LANES = 128
SUBLANES = 8
HALO = 32
VMEM_LIMIT = 56 * 1024 * 1024


def _cparams(*sem):
    return pltpu.CompilerParams(dimension_semantics=sem, vmem_limit_bytes=VMEM_LIMIT)


def _rms(x, g):
    ms = jnp.mean(x * x, axis=-1, keepdims=True)
    return x * lax.rsqrt(ms + RMS_EPS) * g


def _split_bf16(x):
    hi = x.astype(BF16)
    lo = (x - hi.astype(F32)).astype(BF16)
    return hi, lo


def _pack_rows(x, groups):
    w = x.shape[1] // groups
    out = []
    for g in range(groups):
        lo = lax.bitcast_convert_type(x[:, g * w:g * w + w // 2].astype(BF16).astype(F32), U32)
        hi = lax.bitcast_convert_type(x[:, g * w + w // 2:(g + 1) * w].astype(BF16).astype(F32), U32)
        out.append((lo >> 16) | hi)
    return out[0] if groups == 1 else jnp.concatenate(out, axis=1)


def _unpack_rows(p, groups):
    w = p.shape[1] // groups
    out = []
    for g in range(groups):
        pg = p[:, g * w:(g + 1) * w]
        out.append(lax.bitcast_convert_type(pg << 16, F32))
        out.append(lax.bitcast_convert_type(pg & jnp.uint32(0xFFFF0000), F32))
    return jnp.concatenate(out, axis=1)


def _store_tile_rows(ref, idx, packed):
    r = packed.shape[0]
    for q in range(SUBLANES):
        ref[idx + (pl.ds(q, r, stride=SUBLANES), slice(None))] = packed[:, q * LANES:(q + 1) * LANES]


def _load_tile_rows(ref, idx, r):
    return jnp.concatenate([ref[idx + (pl.ds(q, r, stride=SUBLANES), slice(None))] for q in range(SUBLANES)], axis=1)


def _two_part(tm, d, n_a_tiles, col=0):
    return [pl.BlockSpec((tm, d), lambda i: (jnp.minimum(i, n_a_tiles - 1), col)),
            pl.BlockSpec((tm, d), lambda i: (jnp.maximum(i - n_a_tiles, 0), col), pipeline_mode=pl.Buffered(1))]


def _rows(i, n_a_tiles, a_ref, b_ref):
    return jnp.where(i < n_a_tiles, a_ref[...], b_ref[...])


def _norm_mm_kernel(x_ref, g_ref, w_ref, b_ref, o_ref, xn_ref, *, act):
    @pl.when(pl.program_id(1) == 0)
    def _():
        xn_ref[...] = _rms(x_ref[...], g_ref[...]).astype(BF16)

    acc = jnp.dot(xn_ref[...], w_ref[...], preferred_element_type=F32) + b_ref[...]
    if act == "sigmoid":
        acc = jax.nn.sigmoid(acc)
    o_ref[...] = acc.astype(o_ref.dtype)


def _norm_mm(x, g, w, b, *, act, out_dtype, tm, tn):
    n, d = x.shape
    cols = w.shape[1]
    return pl.pallas_call(
        functools.partial(_norm_mm_kernel, act=act),
        grid=(n // tm, cols // tn),
        in_specs=[pl.BlockSpec((tm, d), lambda i, j: (i, 0)),
                  pl.BlockSpec((1, d), lambda i, j: (0, 0)),
                  pl.BlockSpec((d, tn), lambda i, j: (0, j)),
                  pl.BlockSpec((1, tn), lambda i, j: (0, j))],
        out_specs=pl.BlockSpec((tm, tn), lambda i, j: (i, j)),
        out_shape=jax.ShapeDtypeStruct((n, cols), out_dtype),
        scratch_shapes=[pltpu.VMEM((tm, d), BF16)],
        compiler_params=_cparams("parallel", "arbitrary"),
        name="norm_mm_" + str(act),
    )(x, g, w, b)


def _gates_glu_kernel(x_ref, g_ref, wgh_ref, wgl_ref, bg_ref, wa_ref, ba_ref, wb_ref, bb_ref,
                      gate_ref, u_ref, xh_ref):
    @pl.when(pl.program_id(1) == 0)
    def _():
        xn = _rms(x_ref[...], g_ref[...])
        hi, lo = _split_bf16(xn)
        xh_ref[...] = hi
        gate_ref[...] = (jnp.dot(hi, wgh_ref[...], preferred_element_type=F32)
                         + jnp.dot(hi, wgl_ref[...], preferred_element_type=F32)
                         + jnp.dot(lo, wgh_ref[...], preferred_element_type=F32)
                         + bg_ref[...])

    xh = xh_ref[...]
    a = jnp.dot(xh, wa_ref[...], preferred_element_type=F32) + ba_ref[...]
    b = jnp.dot(xh, wb_ref[...], preferred_element_type=F32) + bb_ref[...]
    u_ref[...] = a * jax.nn.sigmoid(b)


def _gates_glu(x, g, wgh, wgl, bg, wa, ba, wb, bb, *, tm, tn):
    n, d = x.shape
    cw = wa.shape[1]
    return pl.pallas_call(
        _gates_glu_kernel,
        grid=(n // tm, cw // tn),
        in_specs=[
            pl.BlockSpec((tm, d), lambda i, j: (i, 0)),
            pl.BlockSpec((1, d), lambda i, j: (0, 0)),
            pl.BlockSpec((d, LANES), lambda i, j: (0, 0)),
            pl.BlockSpec((d, LANES), lambda i, j: (0, 0)),
            pl.BlockSpec((1, LANES), lambda i, j: (0, 0)),
            pl.BlockSpec((d, tn), lambda i, j: (0, j)),
            pl.BlockSpec((1, tn), lambda i, j: (0, j)),
            pl.BlockSpec((d, tn), lambda i, j: (0, j)),
            pl.BlockSpec((1, tn), lambda i, j: (0, j))],
        out_specs=[pl.BlockSpec((tm, LANES), lambda i, j: (i, 0)),
                   pl.BlockSpec((tm, tn), lambda i, j: (i, j))],
        out_shape=[jax.ShapeDtypeStruct((n, LANES), F32),
                   jax.ShapeDtypeStruct((n, cw), F32)],
        scratch_shapes=[pltpu.VMEM((tm, d), BF16)],
        compiler_params=_cparams("parallel", "arbitrary"),
        name="gates_glu",
    )(x, g, wgh, wgl, bg, wa, ba, wb, bb)


def _mlstm_kernel(q_ref, k_ref, v_ref, o_ref, gc_ref, gr_ref, c0_ref, n0_ref, m0_ref,
                  hs_ref, c1_ref, n1_ref, m1_ref, c_sc, n_sc, m_sc, *, heads, dh, scale):
    c = pl.program_id(1)
    nc = pl.num_programs(1)
    L = q_ref.shape[0]

    @pl.when(c == 0)
    def _():
        c_sc[...] = c0_ref[0]
        n_sc[...] = n0_ref[0]
        m_sc[...] = m0_ref[0]

    row = lax.broadcasted_iota(I32, (L, L), 0)
    col = lax.broadcasted_iota(I32, (L, L), 1)
    causal = col <= row
    gc = gc_ref[...]
    gr = gr_ref[0]

    for h in range(heads):
        sl = slice(h * dh, (h + 1) * dh)
        q = q_ref[:, sl]
        ks = k_ref[:, sl] * scale
        v = v_ref[:, sl]
        ig_col = gc[:, h:h + 1]
        lf_col = jax.nn.log_sigmoid(gc[:, heads + h:heads + h + 1])
        ig_row = gr[h:h + 1, :]
        lf_row = jax.nn.log_sigmoid(gr[heads + h:heads + h + 1, :])
        f_col = jnp.sum(jnp.where(causal, lf_row, 0.0), axis=1, keepdims=True)
        f_row = jnp.sum(jnp.where(row <= col, lf_col, 0.0), axis=0, keepdims=True)
        m0 = m_sc[h]
        n0 = n_sc[h]
        c0 = c_sc[h]
        dmat = jnp.where(causal, f_col - f_row + ig_row, -jnp.inf)
        inter = f_col + m0
        m_col = jnp.maximum(inter, jnp.max(dmat, axis=1, keepdims=True))
        a_col = jnp.exp(inter - m_col)
        s = lax.dot_general(q, ks, (((1,), (1,)), ((), ())), preferred_element_type=F32)
        w = jnp.exp(dmat - m_col) * s
        qc = lax.dot_general(q, c0.astype(BF16), (((1,), (1,)), ((), ())), preferred_element_type=F32)
        num = a_col * qc + jnp.dot(w.astype(BF16), v, preferred_element_type=F32)
        qf = q.astype(F32)
        den_raw = a_col * jnp.sum(qf * n0, axis=1, keepdims=True) + jnp.sum(w, axis=1, keepdims=True)
        den = jnp.maximum(jnp.abs(den_raw), jnp.exp(-m_col))
        hval = num / den
        hs_ref[:, sl] = (jax.nn.sigmoid(o_ref[:, sl].astype(F32)) * hval).astype(hs_ref.dtype)
        m_last = m_col[L - 1:L, :]
        f_last = f_col[L - 1:L, :]
        w_end = jnp.exp(f_last - f_col + ig_col - m_last)
        decay = jnp.exp(inter[L - 1:L, :] - m_last)
        vw = (v.astype(F32) * w_end).astype(BF16)
        c_sc[h] = decay * c0 + lax.dot_general(vw, ks, (((0,), (0,)), ((), ())), preferred_element_type=F32)
        n_sc[h] = decay * n0 + jnp.sum(ks.astype(F32) * w_end, axis=0, keepdims=True)
        m_sc[h] = m_last

    @pl.when(c == nc - 1)
    def _():
        c1_ref[0] = c_sc[...]
        n1_ref[0] = n_sc[...]
        m1_ref[0] = m_sc[...]


def _mlstm(qkvo, gates_c, gates_r, c0, n0, m0, *, row0, batch, seq, chunk, heads, dh):
    width = heads * dh
    nc = seq // chunk
    rb0 = row0 // chunk
    n0r = n0.reshape(batch, heads, 1, dh)
    m0r = m0.reshape(batch, heads, 1, 1)

    def blk(part):
        return pl.BlockSpec((chunk, width), lambda b, c: (rb0 + b * nc + c, part))

    state_spec = lambda shp: pl.BlockSpec((1,) + shp, lambda b, c: (b, 0, 0, 0))
    hs, c1, n1, m1 = pl.pallas_call(
        functools.partial(_mlstm_kernel, heads=heads, dh=dh, scale=dh ** -0.5),
        grid=(batch, nc),
        in_specs=[blk(0), blk(1), blk(2), blk(3),
                  pl.BlockSpec((chunk, LANES), lambda b, c: (rb0 + b * nc + c, 0)),
                  pl.BlockSpec((1, 2 * heads, chunk), lambda b, c: (b * nc + c, 0, 0)),
                  state_spec((heads, dh, dh)), state_spec((heads, 1, dh)), state_spec((heads, 1, 1))],
        out_specs=[pl.BlockSpec((chunk, width), lambda b, c: (b * nc + c, 0)),
                   state_spec((heads, dh, dh)), state_spec((heads, 1, dh)), state_spec((heads, 1, 1))],
        out_shape=[jax.ShapeDtypeStruct((batch * seq, width), BF16),
                   jax.ShapeDtypeStruct((batch, heads, dh, dh), F32),
                   jax.ShapeDtypeStruct((batch, heads, 1, dh), F32),
                   jax.ShapeDtypeStruct((batch, heads, 1, 1), F32)],
        scratch_shapes=[pltpu.VMEM((heads, dh, dh), F32), pltpu.VMEM((heads, 1, dh), F32),
                        pltpu.VMEM((heads, 1, 1), F32)],
        compiler_params=_cparams("parallel", "arbitrary"),
        name="mlstm",
    )(qkvo, qkvo, qkvo, qkvo, gates_c, gates_r, c0, n0r, m0r)
    return hs, c1, n1.reshape(batch, heads, dh), m1.reshape(batch, heads)


def _conv_kernel(halo_ref, u_ref, dw_ref, dwb_ref, lg_ref, lb_ref, pw_ref, pwb_ref, o_ref, full_sc, sh_sc, wb_sc, y_sc,
                 *, taps, zero_first, rb, cb):
    tt, cw = u_ref.shape
    halo = halo_ref[...]
    if zero_first:
        halo = jnp.where(pl.program_id(1) == 0, 0.0, halo)
    full_sc[0:HALO, :] = halo
    full_sc[HALO:HALO + tt, :] = u_ref[...]
    span = tt + HALO - SUBLANES
    for s in range(1, SUBLANES):
        sh_sc[s - 1] = full_sc[s:s + span, :]
    for j in range(taps):
        wb_sc[j] = jnp.broadcast_to(dw_ref[j:j + 1, :], (SUBLANES, cw))
    base = HALO - (taps - 1)
    n_acc = 1

    for c0 in range(0, cw, cb):
        cols = slice(c0, c0 + cb)

        def body(r, carry):
            r0 = pl.multiple_of(r * rb, rb)
            accs = [jnp.zeros((rb, cb), F32) + dwb_ref[:, cols]] + [None] * (n_acc - 1)
            for j in range(taps):
                a, s = divmod(base + j, SUBLANES)
                rows = pl.ds(r0 + a * SUBLANES, rb)
                src = full_sc[rows, cols] if s == 0 else sh_sc[s - 1, rows, cols]
                term = jnp.concatenate([wb_sc[j, :, cols]] * (rb // SUBLANES), axis=0) * src
                accs[j % n_acc] = term if accs[j % n_acc] is None else accs[j % n_acc] + term
            y_sc[pl.ds(r0, rb), cols] = functools.reduce(lambda p, q: p + q, accs)
            return carry

        lax.fori_loop(0, tt // rb, body, 0)
    y = y_sc[...]
    mu = jnp.mean(y, axis=-1, keepdims=True)
    yc = y - mu
    yn = yc * lax.rsqrt(jnp.mean(yc * yc, axis=-1, keepdims=True) + LN_EPS) * lg_ref[...] + lb_ref[...]
    act = yn * jax.nn.sigmoid(yn)
    o_ref[...] = (jnp.dot(act.astype(BF16), pw_ref[...], preferred_element_type=F32) + pwb_ref[...]).astype(o_ref.dtype)


def _conv(u, halo_src, dw_w, dw_b, ln_g, ln_b, pw_w, pw_b, *, row0, batch, seq, tt, zero_first):
    cw = u.shape[1]
    d = pw_w.shape[1]
    taps = dw_w.shape[0]
    nt = seq // tt
    tb0 = row0 // tt
    if zero_first:
        hb0, per = row0 // HALO, tt // HALO
        halo_map = lambda b, t: (jnp.maximum(hb0 + (b * nt + t) * per - 1, 0), 0)
    else:
        assert nt == 1, "a carried conv cache is only supported for single-tile sequences"
        halo_map = lambda b, t: (b, 0)
    dw_pad = jnp.zeros((HALO, cw), F32).at[:taps].set(dw_w)
    return pl.pallas_call(
        functools.partial(_conv_kernel, taps=taps, zero_first=zero_first, rb=min(4 * SUBLANES, tt), cb=min(2 * LANES, cw)),
        grid=(batch, nt),
        in_specs=[pl.BlockSpec((HALO, cw), halo_map),
                  pl.BlockSpec((tt, cw), lambda b, t: (tb0 + b * nt + t, 0)),
                  pl.BlockSpec((HALO, cw), lambda b, t: (0, 0)),
                  pl.BlockSpec((1, cw), lambda b, t: (0, 0)),
                  pl.BlockSpec((1, cw), lambda b, t: (0, 0)),
                  pl.BlockSpec((1, cw), lambda b, t: (0, 0)),
                  pl.BlockSpec((cw, d), lambda b, t: (0, 0)),
                  pl.BlockSpec((1, d), lambda b, t: (0, 0))],
        out_specs=pl.BlockSpec((tt, d), lambda b, t: (b * nt + t, 0)),
        out_shape=jax.ShapeDtypeStruct((batch * seq, d), BF16),
        scratch_shapes=[pltpu.VMEM((HALO + tt, cw), F32),
                        pltpu.VMEM((SUBLANES - 1, HALO + tt - SUBLANES, cw), F32),
                        pltpu.VMEM((taps, SUBLANES, cw), F32),
                        pltpu.VMEM((tt, cw), F32)],
        compiler_params=_cparams("parallel", "arbitrary"),
        name="conv_module",
    )(halo_src, u, dw_pad, dw_b, ln_g, ln_b, pw_w, pw_b)


def _merge_kernel(xa_ref, xb_ref, hsa_ref, hsb_ref, boa_ref, bob_ref, gaa_ref, gab_ref, gba_ref, gbb_ref,
                  wa_ref, wo_ref, g_ref, rwh_ref, rwl_ref, rb_ref, x1_ref, xn_ref, lg_ref, *, n_exp, n_a_tiles):
    i = pl.program_id(0)
    a = jnp.dot(_rows(i, n_a_tiles, hsa_ref, hsb_ref), wa_ref[...], preferred_element_type=F32)
    bo = _rows(i, n_a_tiles, boa_ref, bob_ref)
    ga = _rows(i, n_a_tiles, gaa_ref, gab_ref)
    gb = _rows(i, n_a_tiles, gba_ref, gbb_ref)
    mix = ga.astype(F32) * a + gb.astype(F32) * bo.astype(F32)
    x1 = _rows(i, n_a_tiles, xa_ref, xb_ref) + jnp.dot(mix.astype(BF16), wo_ref[...], preferred_element_type=F32)
    x1_ref[...] = x1
    xn = _rms(x1, g_ref[...])
    _store_tile_rows(xn_ref, (), _pack_rows(xn, 1))
    hi, lo = _split_bf16(xn)
    logits = (jnp.dot(hi, rwh_ref[...], preferred_element_type=F32)
              + jnp.dot(hi, rwl_ref[...], preferred_element_type=F32)
              + jnp.dot(lo, rwh_ref[...], preferred_element_type=F32)
              + rb_ref[...])
    lane = lax.broadcasted_iota(I32, logits.shape, 1)
    lg_ref[...] = jnp.where(lane < n_exp, logits, -jnp.inf)


def _merge(xa, xb, hsa, hsb, boa, bob, gaba, gabb, w_a, w_o, g, rwh, rwl, rb, *, n_exp, tm):
    d = xa.shape[1]
    n = xa.shape[0] + xb.shape[0]
    n_a_tiles = xa.shape[0] // tm
    mw = hsa.shape[1]
    const = lambda shp: pl.BlockSpec(shp, lambda i: (0, 0), pipeline_mode=pl.Buffered(1))
    return pl.pallas_call(
        functools.partial(_merge_kernel, n_exp=n_exp, n_a_tiles=n_a_tiles),
        grid=(n // tm,),
        in_specs=_two_part(tm, d, n_a_tiles) + _two_part(tm, mw, n_a_tiles) + _two_part(tm, d, n_a_tiles)
        + _two_part(tm, d, n_a_tiles, 0) + _two_part(tm, d, n_a_tiles, 1) + [
            const((mw, d)), const((d, d)), const((1, d)),
            const((d, LANES)), const((d, LANES)), const((1, LANES))],
        out_specs=[pl.BlockSpec((tm, d), lambda i: (i, 0)),
                   pl.BlockSpec((tm * SUBLANES, LANES), lambda i: (i, 0)),
                   pl.BlockSpec((tm, LANES), lambda i: (i, 0))],
        out_shape=[jax.ShapeDtypeStruct((n, d), F32),
                   jax.ShapeDtypeStruct((n * SUBLANES, LANES), U32),
                   jax.ShapeDtypeStruct((n, LANES), F32)],
        compiler_params=_cparams("parallel"),
        name="merge",
    )(xa, xb, hsa, hsb, boa, bob, gaba, gabb, gaba, gabb, w_a, w_o, g, rwh, rwl, rb)


def _router_kernel(lg_ref, w_ref, e_ref, r_ref, cnt_ref, carry_sc):
    i = pl.program_id(0)

    @pl.when(i == 0)
    def _():
        carry_sc[...] = jnp.zeros_like(carry_sc)

    l = lg_ref[...]
    tm = l.shape[0]
    lane = lax.broadcasted_iota(I32, l.shape, 1)
    lane_f = lane.astype(F32)
    vals, hots, idxs = [], [], []
    for _ in range(TOP_K):
        mx = jnp.max(l, axis=1, keepdims=True)
        idx = jnp.min(jnp.where(l == mx, lane_f, float(LANES)), axis=1, keepdims=True)
        hot = lane_f == idx
        vals.append(mx)
        hots.append(hot)
        idxs.append(idx)
        l = jnp.where(hot, -jnp.inf, l)
    ex = [jnp.exp(v - vals[0]) for v in vals]
    tot = ex[0] + ex[1] + ex[2] + ex[3]
    sel = jnp.zeros(l.shape, F32)
    for hot in hots:
        sel = jnp.where(hot, 1.0, sel)
    rr = lax.broadcasted_iota(I32, (tm, tm), 0)
    cc = lax.broadcasted_iota(I32, (tm, tm), 1)
    tri = jnp.where(cc < rr, 1.0, 0.0).astype(BF16)
    cum = jnp.dot(tri, sel.astype(BF16), preferred_element_type=F32) + carry_sc[...]
    w_out = jnp.zeros(l.shape, F32)
    e_out = jnp.zeros(l.shape, I32)
    r_out = jnp.zeros(l.shape, I32)
    for k in range(TOP_K):
        rank = jnp.sum(jnp.where(hots[k], cum, 0.0), axis=1, keepdims=True).astype(I32)
        w_out = jnp.where(lane == k, ex[k] / tot, w_out)
        e_out = jnp.where(lane == k, idxs[k].astype(I32), e_out)
        r_out = jnp.where(lane == k, rank, r_out)
    w_ref[...] = w_out
    e_ref[...] = e_out
    r_ref[...] = r_out
    carry_sc[...] = carry_sc[...] + jnp.sum(sel, axis=0, keepdims=True)
    cnt_ref[...] = carry_sc[...]


def _router(logits, *, tm):
    n = logits.shape[0]
    blk = pl.BlockSpec((tm, LANES), lambda i: (i, 0))
    return pl.pallas_call(
        _router_kernel,
        grid=(n // tm,),
        in_specs=[blk],
        out_specs=[blk, blk, blk, pl.BlockSpec((1, LANES), lambda i: (0, 0))],
        out_shape=[jax.ShapeDtypeStruct((n, LANES), F32), jax.ShapeDtypeStruct((n, LANES), I32),
                   jax.ShapeDtypeStruct((n, LANES), I32), jax.ShapeDtypeStruct((1, LANES), F32)],
        scratch_shapes=[pltpu.VMEM((1, LANES), F32)],
        compiler_params=_cparams("arbitrary"),
        name="router",
    )(logits)


def _dispatch_kernel(pad_ref, slot_ref, x_ref, xs_ref, zero_sc, sem, zsem, *, n_exp, n_tiles):
    tm = x_ref.shape[0] // SUBLANES
    tme = zero_sc.shape[0] // SUBLANES

    def tile_rows(row, size):
        return pl.ds(pl.multiple_of(row * SUBLANES, SUBLANES), size * SUBLANES)

    def zero_fill(wait):
        def run(row, size):
            cp = pltpu.make_async_copy(zero_sc.at[pl.ds(0, size * SUBLANES)], xs_ref.at[tile_rows(row, size)], zsem)
            if wait:
                cp.wait()
            else:
                cp.start()

        def per_expert(e, carry):
            first, length = pad_ref[e], pad_ref[n_exp + e]
            size = tme // 2
            while size >= 1:
                @pl.when((length & size) != 0)
                def _(size=size):
                    run(first + (length & ~(2 * size - 1)), size)
                size //= 2
            return carry

        def per_tile(t, carry):
            run(t * tme, tme)
            return carry

        lax.fori_loop(0, n_exp, per_expert, 0)
        lax.fori_loop(pad_ref[2 * n_exp], n_tiles, per_tile, 0)

    @pl.when(pl.program_id(0) == 0)
    def _():
        zero_sc[...] = jnp.zeros_like(zero_sc)
        zero_fill(wait=False)
        zero_fill(wait=True)

    def row_copy(r, k):
        return pltpu.make_async_copy(x_ref.at[tile_rows(r, 1)], xs_ref.at[tile_rows(slot_ref[0, 0, r * TOP_K + k], 1)], sem)

    def start_rows(r, carry):
        for k in range(TOP_K):
            row_copy(r, k).start(priority=k % 2)
        return carry

    def wait_rows(r, carry):
        for k in range(TOP_K):
            row_copy(r, k).wait()
        return carry

    lax.fori_loop(0, tm, start_rows, 0)
    lax.fori_loop(0, tm, wait_rows, 0)


def _dispatch(pads, slots, xn, n_tiles, tme, *, n_exp, tm):
    n = xn.shape[0] // SUBLANES
    n_slots = n_tiles * tme
    return pl.pallas_call(
        functools.partial(_dispatch_kernel, n_exp=n_exp, n_tiles=n_tiles),
        grid_spec=pltpu.PrefetchScalarGridSpec(
            num_scalar_prefetch=1,
            grid=(n // tm,),
            in_specs=[pl.BlockSpec((1, 1, tm * TOP_K), lambda i, pads: (i, 0, 0), memory_space=pltpu.SMEM),
                      pl.BlockSpec((tm * SUBLANES, LANES), lambda i, pads: (i, 0))],
            out_specs=pl.BlockSpec(memory_space=pl.ANY),
            scratch_shapes=[pltpu.VMEM((tme * SUBLANES, LANES), xn.dtype), pltpu.SemaphoreType.DMA(()),
                            pltpu.SemaphoreType.DMA(())]),
        out_shape=jax.ShapeDtypeStruct((n_slots * SUBLANES, LANES), xn.dtype),
        compiler_params=_cparams("arbitrary"),
        name="dispatch",
    )(pads, slots.reshape(n // tm, 1, tm * TOP_K), xn)


def _tile_state(te_ref, axis):
    i = pl.program_id(axis)
    used = i < te_ref[pl.num_programs(axis)]
    new_weights = jnp.logical_or(i == 0, te_ref[i] != te_ref[jnp.maximum(i - 1, 0)])
    return used, new_weights


def _stream_weights(te_ref, nxt_ref, i, j, nj, used, new_weights, weights, sem):
    tn = weights[0][1].shape[1]

    def copies(e, jj):
        cols = pl.ds(jj * tn if isinstance(jj, int) else pl.multiple_of(jj * tn, tn), tn)
        return [pltpu.make_async_copy(w.at[e, :, cols], stage, sem.at[s]) for s, (w, stage, _) in enumerate(weights)]

    @pl.when(jnp.logical_and(used, new_weights))
    def _():
        @pl.when(jnp.logical_and(i == 0, j == 0))
        def _():
            for c in copies(te_ref[0], 0):
                c.start()

        for c in copies(te_ref[i], j):
            c.wait()
        for _, stage, dst in weights:
            dst[...] = stage[...].astype(BF16)
        nxt = nxt_ref[i]

        @pl.when(nxt >= 0)
        def _():
            for c in copies(nxt, j):
                c.start()

        if not (isinstance(j, int) and j + 1 >= nj):
            @pl.when(jnp.logical_and(nxt < 0, j + 1 < nj))
            def _():
                for c in copies(te_ref[0], j + 1):
                    c.start()


def _up_kernel(te_ref, nxt_ref, xs_ref, wg_ref, bg_ref, wu_ref, bu_ref, h_ref, gf_sc, uf_sc, wg_sc, wu_sc, sem):
    used, new_weights = _tile_state(te_ref, 1)
    _stream_weights(te_ref, nxt_ref, pl.program_id(1), pl.program_id(0), pl.num_programs(0), used, new_weights,
                    [(wg_ref, gf_sc, wg_sc), (wu_ref, uf_sc, wu_sc)], sem)

    @pl.when(used)
    def _():
        x = _unpack_rows(_load_tile_rows(xs_ref, (), h_ref.shape[0]), 1).astype(BF16)
        g = jnp.minimum(jnp.dot(x, wg_sc[...], preferred_element_type=F32) + bg_ref[0], SWIGLU_LIMIT)
        u = jnp.clip(jnp.dot(x, wu_sc[...], preferred_element_type=F32) + bu_ref[0], -SWIGLU_LIMIT, SWIGLU_LIMIT)
        h_ref[...] = (g * jax.nn.sigmoid(SWIGLU_ALPHA * g) * (u + 1.0)).astype(h_ref.dtype)

    @pl.when(jnp.logical_not(used))
    def _():
        h_ref[...] = jnp.zeros_like(h_ref)


def _down_kernel(te_ref, nxt_ref, h_ref, wd_ref, bd_ref, o_ref, df_sc, wd_sc, sem):
    used, new_weights = _tile_state(te_ref, 0)
    _stream_weights(te_ref, nxt_ref, pl.program_id(0), 0, 1, used, new_weights, [(wd_ref, df_sc, wd_sc)], sem)

    @pl.when(used)
    def _():
        out = jnp.dot(h_ref[...], wd_sc[...], preferred_element_type=F32) + bd_ref[0]
        _store_tile_rows(o_ref, (), _pack_rows(out, 1))

    @pl.when(jnp.logical_not(used))
    def _():
        o_ref[...] = jnp.zeros_like(o_ref)


def _used_tile(i, te):
    return jnp.where(i < te[te.shape[0] - 1], i, 0)


def _expert_up(tile_exp, next_exp, xs, wg, bg, wu, bu, *, tm, tn):
    s = xs.shape[0] // SUBLANES
    d, de = wg.shape[1], wg.shape[2]
    bspec = pl.BlockSpec((1, 1, tn), lambda j, i, te, nx: (te[i], 0, j))
    hbm = pl.BlockSpec(memory_space=pl.ANY)
    return pl.pallas_call(
        _up_kernel,
        grid_spec=pltpu.PrefetchScalarGridSpec(
            num_scalar_prefetch=2,
            grid=(de // tn, s // tm),
            in_specs=[pl.BlockSpec((tm * SUBLANES, LANES), lambda j, i, te, nx: (_used_tile(i, te), 0)),
                      hbm, bspec, hbm, bspec],
            out_specs=pl.BlockSpec((tm, tn), lambda j, i, te, nx: (i, j)),
            scratch_shapes=[pltpu.VMEM((d, tn), F32), pltpu.VMEM((d, tn), F32),
                            pltpu.VMEM((d, tn), BF16), pltpu.VMEM((d, tn), BF16),
                            pltpu.SemaphoreType.DMA((2,))]),
        out_shape=jax.ShapeDtypeStruct((s, de), BF16),
        compiler_params=_cparams("arbitrary", "arbitrary"),
        name="expert_up",
    )(tile_exp, next_exp, xs, wg, bg, wu, bu)


def _expert_down(tile_exp, next_exp, hid, wd, bd, *, tm):
    s, de = hid.shape
    d = wd.shape[2]
    return pl.pallas_call(
        _down_kernel,
        grid_spec=pltpu.PrefetchScalarGridSpec(
            num_scalar_prefetch=2,
            grid=(s // tm,),
            in_specs=[pl.BlockSpec((tm, de), lambda i, te, nx: (_used_tile(i, te), 0)),
                      pl.BlockSpec(memory_space=pl.ANY),
                      pl.BlockSpec((1, 1, d), lambda i, te, nx: (te[i], 0, 0))],
            out_specs=pl.BlockSpec((tm * SUBLANES, LANES), lambda i, te, nx: (i, 0)),
            scratch_shapes=[pltpu.VMEM((de, d), F32), pltpu.VMEM((de, d), BF16), pltpu.SemaphoreType.DMA((1,))]),
        out_shape=jax.ShapeDtypeStruct((s * SUBLANES, LANES), U32),
        compiler_params=_cparams("arbitrary"),
        name="expert_down",
    )(tile_exp, next_exp, hid, wd, bd)


def _combine_kernel(slot_ref, nslot_ref, x1_ref, tw_ref, eo_ref, p_ref, gp_ref, wpg_ref, wple_ref, gf_ref, y_ref,
                    rows_sc, sem, *, nt):
    tm = x1_ref.shape[0]
    i = pl.program_id(0)

    def tile_rows(row):
        return pl.ds(pl.multiple_of(row * SUBLANES, SUBLANES), SUBLANES)

    def row_copy(ref, buf, r, k):
        return pltpu.make_async_copy(eo_ref.at[tile_rows(ref[0, 0, r * TOP_K + k])],
                                     rows_sc.at[buf, k, tile_rows(r)], sem.at[buf])

    def start_tile(ref, buf):
        def body(r, carry):
            for k in range(TOP_K):
                row_copy(ref, buf, r, k).start(priority=k % 2)
            return carry
        lax.fori_loop(0, tm, body, 0)

    @pl.when(i == 0)
    def _():
        start_tile(slot_ref, 0)

    @pl.when(i + 1 < nt)
    def _():
        start_tile(nslot_ref, (i + 1) % 2)

    buf = i % 2

    def wait_body(r, carry):
        for k in range(TOP_K):
            row_copy(slot_ref, buf, r, k).wait()
        return carry

    lax.fori_loop(0, tm, wait_body, 0)
    tw = tw_ref[...]
    moe = jnp.zeros(x1_ref.shape, F32)
    for k in range(TOP_K):
        moe = moe + tw[:, k:k + 1] * _unpack_rows(_load_tile_rows(rows_sc, (buf, k), tm), 1)
    x2 = x1_ref[...] + moe
    gate = jax.nn.sigmoid(jnp.dot(_rms(x2, gp_ref[...]).astype(BF16), wpg_ref[...], preferred_element_type=F32))
    ple = jnp.dot(p_ref[...].astype(BF16), wple_ref[...], preferred_element_type=F32)
    x3 = x2 + ple * gate
    y_ref[...] = _rms(x3, gf_ref[...])


def _combine(slots, x1, topw, eo, p, g_ple, w_pg, w_ple, g_final, *, row0, tm):
    n, d = x1.shape
    n_part, pd = p.shape
    nt = n_part // tm
    t0 = row0 // tm
    const = lambda shp: pl.BlockSpec(shp, lambda i: (0, 0), pipeline_mode=pl.Buffered(1))
    slots3 = slots.reshape(n // tm, 1, tm * TOP_K)
    return pl.pallas_call(
        functools.partial(_combine_kernel, nt=nt),
        grid=(nt,),
        in_specs=[pl.BlockSpec((1, 1, tm * TOP_K), lambda i: (t0 + i, 0, 0), memory_space=pltpu.SMEM),
                  pl.BlockSpec((1, 1, tm * TOP_K), lambda i: (t0 + jnp.minimum(i + 1, nt - 1), 0, 0),
                               memory_space=pltpu.SMEM),
                  pl.BlockSpec((tm, d), lambda i: (t0 + i, 0)),
                  pl.BlockSpec((tm, LANES), lambda i: (t0 + i, 0)),
                  pl.BlockSpec(memory_space=pl.ANY),
                  pl.BlockSpec((tm, pd), lambda i: (i, 0)),
                  const((1, d)), const((d, d)), const((pd, d)), const((1, d))],
        out_specs=pl.BlockSpec((tm, d), lambda i: (i, 0)),
        out_shape=jax.ShapeDtypeStruct((n_part, d), F32),
        scratch_shapes=[pltpu.VMEM((2, TOP_K, tm * SUBLANES, LANES), eo.dtype), pltpu.SemaphoreType.DMA((2,))],
        compiler_params=_cparams("arbitrary"),
        name="combine",
    )(slots3, slots3, x1, topw, eo, p, g_ple, w_pg, w_ple, g_final)


def _pick(n, pref):
    t = pref
    while n % t:
        t //= 2
    return t


def _layer(xa, xb, pa, pb, seqs, state, params):
    (g_mix, w_in, b_in, w_a, dw_w, dw_b, ln_g, ln_b, pw_w, pw_b, w_o, g_ffn,
     r_w, r_b, wg, bg, wu, bu, wd, bd, g_ple, w_ple, w_pg, g_final) = params
    d = xa.shape[1]
    n = xa.shape[0] + xb.shape[0]
    both = math.gcd(xa.shape[0], xb.shape[0])
    heads = state[0][1].shape[1]
    dh = state[0][1].shape[2]
    mw = heads * dh
    cw = dw_w.shape[1]
    taps = dw_w.shape[0]
    n_exp = r_w.shape[1]
    de = wg.shape[2]
    row = lambda v: v.reshape(1, -1)

    o_g = 4 * mw
    o_glu = o_g + 2 * heads
    o_gate = o_glu + 2 * cw
    w_qkvo = w_in[:, :o_g].astype(BF16)
    w_gt = jnp.zeros((d, LANES), F32).at[:, :2 * heads].set(w_in[:, o_g:o_glu])
    b_gt = jnp.zeros((1, LANES), F32).at[:, :2 * heads].set(b_in[o_g:o_glu])
    w_gt_hi, w_gt_lo = _split_bf16(w_gt)
    w_ga = w_in[:, o_glu:o_glu + cw].astype(BF16)
    w_gb = w_in[:, o_glu + cw:o_gate].astype(BF16)
    w_gab = w_in[:, o_gate:].astype(BF16)

    hs, bo, gab, new_state = [], [], [], []
    pw_bf = pw_w.astype(BF16)
    for x, (batch, seq, zero_state), (c0, n0, m0, buf) in zip((xa, xb), seqs, state):
        tm = _pick(x.shape[0], 1024)
        qkvo = _norm_mm(x, row(g_mix), w_qkvo, row(b_in[:o_g]), act=None, out_dtype=BF16, tm=tm, tn=_pick(o_g, 1024))
        gab_i = _norm_mm(x, row(g_mix), w_gab, row(b_in[o_gate:]), act="sigmoid", out_dtype=BF16, tm=tm,
                         tn=_pick(2 * d, 1024))
        gates, u = _gates_glu(x, row(g_mix), w_gt_hi, w_gt_lo, b_gt, w_ga, row(b_in[o_glu:o_glu + cw]),
                              w_gb, row(b_in[o_glu + cw:o_gate]), tm=_pick(x.shape[0], 512), tn=_pick(cw, 512))
        chunk = seq
        for cand in (256, 128, 64):
            if seq % CHUNK == 0 and seq % cand == 0:
                chunk = cand
                break
        gates_r = gates[:, :2 * heads].reshape(-1, chunk, 2 * heads).transpose(0, 2, 1)
        hs_i, c1, n1, m1 = _mlstm(qkvo, gates, gates_r, c0, n0, m0, row0=0, batch=batch, seq=seq,
                                  chunk=chunk, heads=heads, dh=dh)
        if zero_state:
            halo_src = u
        else:
            halo_src = jnp.pad(buf, ((0, 0), (HALO - (taps - 1), 0), (0, 0))).reshape(batch * HALO, cw)
        bo_i = _conv(u, halo_src, dw_w, row(dw_b), row(ln_g), row(ln_b), pw_bf, row(pw_b),
                     row0=0, batch=batch, seq=seq, tt=_pick(seq, 256), zero_first=zero_state)
        hs.append(hs_i)
        bo.append(bo_i)
        gab.append(gab_i)
        u_seq = u.reshape(batch, seq, cw)
        if seq >= taps - 1:
            buf1 = u_seq[:, seq - (taps - 1):]
        else:
            buf1 = jnp.concatenate([buf, u_seq], axis=1)[:, -(taps - 1):]
        new_state.append((c1, n1, m1, buf1))

    r_pad = jnp.zeros((d, LANES), F32).at[:, :n_exp].set(r_w)
    rb_pad = jnp.zeros((1, LANES), F32).at[:, :n_exp].set(r_b)
    r_hi, r_lo = _split_bf16(r_pad)
    tmd = _pick(both, 256)
    assert d // 2 == SUBLANES * LANES, "a packed token row must fill exactly one sublane tile"
    x1, xn2, logits = _merge(xa, xb, hs[0], hs[1], bo[0], bo[1], gab[0], gab[1], w_a.astype(BF16), w_o.astype(BF16),
                             row(g_ffn), r_hi, r_lo, rb_pad, n_exp=n_exp, tm=tmd)
    topw, tope, rank, counts = _router(logits, tm=_pick(n, 512))

    tme = _pick(n * TOP_K, 512)
    cnt = counts[0, :n_exp].astype(I32)
    tiles_per = (cnt + tme - 1) // tme
    tile_end = jnp.cumsum(tiles_per)
    offs = (tile_end - tiles_per) * tme
    n_tiles = (n * TOP_K) // tme + n_exp
    tile_ids = jnp.arange(n_tiles, dtype=I32)
    tile_exp = jnp.minimum(jnp.sum(tile_end[None, :] <= tile_ids[:, None], axis=1), n_exp - 1)
    exp_ids = jnp.arange(n_exp, dtype=I32)
    later_active = jnp.logical_and(exp_ids[None, :] > exp_ids[:, None], tiles_per[None, :] > 0)
    next_active = jnp.min(jnp.where(later_active, exp_ids[None, :], n_exp), axis=1)
    next_active = jnp.where(next_active == n_exp, -1, next_active)
    next_exp = jnp.sum(jnp.where(tile_exp[:, None] == exp_ids[None, :], next_active[None, :], 0), axis=1).astype(I32)
    tile_exp = jnp.concatenate([tile_exp.astype(I32), tile_end[-1:].astype(I32)])
    pads = jnp.concatenate([offs + cnt, tiles_per * tme - cnt, tile_end[-1:]]).astype(I32)
    slots = rank[:, :TOP_K] + jnp.sum(jnp.where(tope[:, :TOP_K, None] == exp_ids, offs, 0), axis=-1)

    xs = _dispatch(pads, slots.astype(I32), xn2, n_tiles, tme, n_exp=n_exp, tm=tmd)
    hid = _expert_up(tile_exp, next_exp, xs, wg, bg[:, None, :], wu, bu[:, None, :], tm=tme, tn=_pick(de, 1024))
    eo = _expert_down(tile_exp, next_exp, hid, wd, bd[:, None, :], tm=tme)
    w_pg_bf, w_ple_bf = w_pg.astype(BF16), w_ple.astype(BF16)
    ys = [_combine(slots.astype(I32), x1, topw, eo, p, row(g_ple), w_pg_bf, w_ple_bf, row(g_final), row0=r0, tm=tmd)
          for r0, p in ((0, pa), (xa.shape[0], pb))]
    return ys, new_state


def kernel(x_prompt, x_sample, state_mlstm_C, state_mlstm_n, state_mlstm_m, cache_conv, p_prompt, p_sample, norm_mix, w_in, b_in, w_a, conv_dw_w, conv_dw_b, conv_ln_g, conv_ln_b, conv_pw_w, conv_pw_b, w_o, norm_ffn, router_w, router_b, exp_w_gate, exp_b_gate, exp_w_up, exp_b_up, exp_w_down, exp_b_down, norm_ple, w_ple, w_ple_gate, norm_final):
    depth = w_in.shape[0]
    assert depth == 1, "the final RMSNorm is fused into the single layer's last kernel"
    bp, sp, d = x_prompt.shape
    bs, ss, _ = x_sample.shape
    heads, dh = state_mlstm_C.shape[2], state_mlstm_C.shape[3]
    cw = cache_conv.shape[3]
    taps = conv_dw_w.shape[1]
    n_p, n_s = bp * sp, bs * ss
    l = 0
    zero_state = (jnp.zeros((bp, heads, dh, dh), F32), jnp.zeros((bp, heads, dh), F32),
                  jnp.zeros((bp, heads), F32), jnp.zeros((bp, taps - 1, cw), F32))
    carried = (state_mlstm_C[l].astype(F32), state_mlstm_n[l].astype(F32), state_mlstm_m[l].astype(F32), cache_conv[l])
    params = (norm_mix[l], w_in[l], b_in[l], w_a[l], conv_dw_w[l], conv_dw_b[l], conv_ln_g[l], conv_ln_b[l],
              conv_pw_w[l], conv_pw_b[l], w_o[l], norm_ffn[l], router_w[l], router_b[l],
              exp_w_gate[l], exp_b_gate[l], exp_w_up[l], exp_b_up[l], exp_w_down[l], exp_b_down[l],
              norm_ple[l], w_ple[l], w_ple_gate[l], norm_final)
    seqs = [(bp, sp, True), (bs, ss, False)]
    (y_p, y_s), ((c_p, n_pp, m_p, b_p), (c_s, n_sp, m_s, b_s)) = _layer(
        x_prompt.reshape(n_p, d), x_sample.reshape(n_s, d), p_prompt[l].reshape(n_p, -1), p_sample[l].reshape(n_s, -1),
        seqs, [zero_state, carried], params)
    st = lambda a: a[None]
    return (y_p.reshape(bp, sp, d), y_s.reshape(bs, ss, d), st(c_p), st(n_pp), st(m_p), st(b_p),
            st(c_s), st(n_sp), st(m_s), st(b_s))
```

```python
import functools
import math

import jax
import jax.numpy as jnp
from jax import lax
from jax.experimental import pallas as pl
from jax.experimental.pallas import tpu as pltpu

F32 = jnp.float32
BF16 = jnp.bfloat16
I32 = jnp.int32
U32 = jnp.uint32

RMS_EPS = 1e-6
LN_EPS = 1e-5
TOP_K = 4
SWIGLU_ALPHA = 1.702
SWIGLU_LIMIT = 7.0
CHUNK = 64
LANES = 128
SUBLANES = 8
HALO = 32
ROW_LOOP_UNROLL = 8
VMEM_LIMIT = 56 * 1024 * 1024


def _cparams(*sem):
    return pltpu.CompilerParams(dimension_semantics=sem, vmem_limit_bytes=VMEM_LIMIT)


def _rms(x, g):
    ms = jnp.mean(x * x, axis=-1, keepdims=True)
    return x * lax.rsqrt(ms + RMS_EPS) * g


def _split_bf16(x):
    hi = x.astype(BF16)
    lo = (x - hi.astype(F32)).astype(BF16)
    return hi, lo


def _pack_rows(x, groups):
    w = x.shape[1] // groups
    out = []
    for g in range(groups):
        lo = lax.bitcast_convert_type(x[:, g * w:g * w + w // 2].astype(BF16).astype(F32), U32)
        hi = lax.bitcast_convert_type(x[:, g * w + w // 2:(g + 1) * w].astype(BF16).astype(F32), U32)
        out.append((lo >> 16) | hi)
    return out[0] if groups == 1 else jnp.concatenate(out, axis=1)


def _unpack_rows(p, groups):
    w = p.shape[1] // groups
    out = []
    for g in range(groups):
        pg = p[:, g * w:(g + 1) * w]
        out.append(lax.bitcast_convert_type(pg << 16, F32))
        out.append(lax.bitcast_convert_type(pg & jnp.uint32(0xFFFF0000), F32))
    return jnp.concatenate(out, axis=1)


def _store_tile_rows(ref, idx, packed):
    r = packed.shape[0]
    for q in range(SUBLANES):
        ref[idx + (pl.ds(q, r, stride=SUBLANES), slice(None))] = packed[:, q * LANES:(q + 1) * LANES]


def _load_tile_rows(ref, idx, r):
    return jnp.concatenate([ref[idx + (pl.ds(q, r, stride=SUBLANES), slice(None))] for q in range(SUBLANES)], axis=1)


def _two_part(tm, d, n_a_tiles, col=0):
    return [pl.BlockSpec((tm, d), lambda i: (jnp.minimum(i, n_a_tiles - 1), col)),
            pl.BlockSpec((tm, d), lambda i: (jnp.maximum(i - n_a_tiles, 0), col), pipeline_mode=pl.Buffered(1))]


def _rows(i, n_a_tiles, a_ref, b_ref):
    return jnp.where(i < n_a_tiles, a_ref[...], b_ref[...])


def _norm_mm_kernel(x_ref, g_ref, w_ref, b_ref, o_ref, xn_ref, *, act):
    @pl.when(pl.program_id(1) == 0)
    def _():
        xn_ref[...] = _rms(x_ref[...], g_ref[...]).astype(BF16)

    acc = jnp.dot(xn_ref[...], w_ref[...], preferred_element_type=F32) + b_ref[...]
    if act == "sigmoid":
        acc = jax.nn.sigmoid(acc)
    o_ref[...] = acc.astype(o_ref.dtype)


def _norm_mm(x, g, w, b, *, act, out_dtype, tm, tn):
    n, d = x.shape
    cols = w.shape[1]
    return pl.pallas_call(
        functools.partial(_norm_mm_kernel, act=act),
        grid=(n // tm, cols // tn),
        in_specs=[pl.BlockSpec((tm, d), lambda i, j: (i, 0)),
                  pl.BlockSpec((1, d), lambda i, j: (0, 0)),
                  pl.BlockSpec((d, tn), lambda i, j: (0, j)),
                  pl.BlockSpec((1, tn), lambda i, j: (0, j))],
        out_specs=pl.BlockSpec((tm, tn), lambda i, j: (i, j)),
        out_shape=jax.ShapeDtypeStruct((n, cols), out_dtype),
        scratch_shapes=[pltpu.VMEM((tm, d), BF16)],
        compiler_params=_cparams("parallel", "arbitrary"),
        name="norm_mm_" + str(act),
    )(x, g, w, b)


def _gates_glu_kernel(x_ref, g_ref, wgh_ref, wgl_ref, bg_ref, wa_ref, ba_ref, wb_ref, bb_ref,
                      gate_ref, u_ref, xh_ref):
    @pl.when(pl.program_id(1) == 0)
    def _():
        xn = _rms(x_ref[...], g_ref[...])
        hi, lo = _split_bf16(xn)
        xh_ref[...] = hi
        gate_ref[...] = (jnp.dot(hi, wgh_ref[...], preferred_element_type=F32)
                         + jnp.dot(hi, wgl_ref[...], preferred_element_type=F32)
                         + jnp.dot(lo, wgh_ref[...], preferred_element_type=F32)
                         + bg_ref[...])

    xh = xh_ref[...]
    a = jnp.dot(xh, wa_ref[...], preferred_element_type=F32) + ba_ref[...]
    b = jnp.dot(xh, wb_ref[...], preferred_element_type=F32) + bb_ref[...]
    u_ref[...] = a * jax.nn.sigmoid(b)


def _gates_glu(x, g, wgh, wgl, bg, wa, ba, wb, bb, *, tm, tn):
    n, d = x.shape
    cw = wa.shape[1]
    return pl.pallas_call(
        _gates_glu_kernel,
        grid=(n // tm, cw // tn),
        in_specs=[
            pl.BlockSpec((tm, d), lambda i, j: (i, 0)),
            pl.BlockSpec((1, d), lambda i, j: (0, 0)),
            pl.BlockSpec((d, LANES), lambda i, j: (0, 0)),
            pl.BlockSpec((d, LANES), lambda i, j: (0, 0)),
            pl.BlockSpec((1, LANES), lambda i, j: (0, 0)),
            pl.BlockSpec((d, tn), lambda i, j: (0, j)),
            pl.BlockSpec((1, tn), lambda i, j: (0, j)),
            pl.BlockSpec((d, tn), lambda i, j: (0, j)),
            pl.BlockSpec((1, tn), lambda i, j: (0, j))],
        out_specs=[pl.BlockSpec((tm, LANES), lambda i, j: (i, 0)),
                   pl.BlockSpec((tm, tn), lambda i, j: (i, j))],
        out_shape=[jax.ShapeDtypeStruct((n, LANES), F32),
                   jax.ShapeDtypeStruct((n, cw), F32)],
        scratch_shapes=[pltpu.VMEM((tm, d), BF16)],
        compiler_params=_cparams("parallel", "arbitrary"),
        name="gates_glu",
    )(x, g, wgh, wgl, bg, wa, ba, wb, bb)


def _mlstm_kernel(q_ref, k_ref, v_ref, o_ref, gc_ref, gr_ref, c0_ref, n0_ref, m0_ref,
                  hs_ref, c1_ref, n1_ref, m1_ref, c_sc, n_sc, m_sc, *, heads, dh, scale):
    c = pl.program_id(1)
    nc = pl.num_programs(1)
    L = q_ref.shape[0]

    @pl.when(c == 0)
    def _():
        c_sc[...] = c0_ref[0]
        n_sc[...] = n0_ref[0]
        m_sc[...] = m0_ref[0]

    row = lax.broadcasted_iota(I32, (L, L), 0)
    col = lax.broadcasted_iota(I32, (L, L), 1)
    causal = col <= row
    gc = gc_ref[...]
    gr = gr_ref[0]

    for h in range(heads):
        sl = slice(h * dh, (h + 1) * dh)
        q = q_ref[:, sl]
        ks = k_ref[:, sl] * scale
        v = v_ref[:, sl]
        ig_col = gc[:, h:h + 1]
        lf_col = jax.nn.log_sigmoid(gc[:, heads + h:heads + h + 1])
        ig_row = gr[h:h + 1, :]
        lf_row = jax.nn.log_sigmoid(gr[heads + h:heads + h + 1, :])
        f_col = jnp.sum(jnp.where(causal, lf_row, 0.0), axis=1, keepdims=True)
        f_row = jnp.sum(jnp.where(row <= col, lf_col, 0.0), axis=0, keepdims=True)
        m0 = m_sc[h]
        n0 = n_sc[h]
        c0 = c_sc[h]
        dmat = jnp.where(causal, f_col - f_row + ig_row, -jnp.inf)
        inter = f_col + m0
        m_col = jnp.maximum(inter, jnp.max(dmat, axis=1, keepdims=True))
        a_col = jnp.exp(inter - m_col)
        s = lax.dot_general(q, ks, (((1,), (1,)), ((), ())), preferred_element_type=F32)
        w = jnp.exp(dmat - m_col) * s
        qc = lax.dot_general(q, c0.astype(BF16), (((1,), (1,)), ((), ())), preferred_element_type=F32)
        num = a_col * qc + jnp.dot(w.astype(BF16), v, preferred_element_type=F32)
        qf = q.astype(F32)
        den_raw = a_col * jnp.sum(qf * n0, axis=1, keepdims=True) + jnp.sum(w, axis=1, keepdims=True)
        den = jnp.maximum(jnp.abs(den_raw), jnp.exp(-m_col))
        hval = num / den
        hs_ref[:, sl] = (jax.nn.sigmoid(o_ref[:, sl].astype(F32)) * hval).astype(hs_ref.dtype)
        m_last = m_col[L - 1:L, :]
        f_last = f_col[L - 1:L, :]
        w_end = jnp.exp(f_last - f_col + ig_col - m_last)
        decay = jnp.exp(inter[L - 1:L, :] - m_last)
        vw = (v.astype(F32) * w_end).astype(BF16)
        c_sc[h] = decay * c0 + lax.dot_general(vw, ks, (((0,), (0,)), ((), ())), preferred_element_type=F32)
        n_sc[h] = decay * n0 + jnp.sum(ks.astype(F32) * w_end, axis=0, keepdims=True)
        m_sc[h] = m_last

    @pl.when(c == nc - 1)
    def _():
        c1_ref[0] = c_sc[...]
        n1_ref[0] = n_sc[...]
        m1_ref[0] = m_sc[...]


def _mlstm(qkvo, gates_c, gates_r, c0, n0, m0, *, row0, batch, seq, chunk, heads, dh):
    width = heads * dh
    nc = seq // chunk
    rb0 = row0 // chunk
    n0r = n0.reshape(batch, heads, 1, dh)
    m0r = m0.reshape(batch, heads, 1, 1)

    def blk(part):
        return pl.BlockSpec((chunk, width), lambda b, c: (rb0 + b * nc + c, part))

    state_spec = lambda shp: pl.BlockSpec((1,) + shp, lambda b, c: (b, 0, 0, 0))
    hs, c1, n1, m1 = pl.pallas_call(
        functools.partial(_mlstm_kernel, heads=heads, dh=dh, scale=dh ** -0.5),
        grid=(batch, nc),
        in_specs=[blk(0), blk(1), blk(2), blk(3),
                  pl.BlockSpec((chunk, LANES), lambda b, c: (rb0 + b * nc + c, 0)),
                  pl.BlockSpec((1, 2 * heads, chunk), lambda b, c: (b * nc + c, 0, 0)),
                  state_spec((heads, dh, dh)), state_spec((heads, 1, dh)), state_spec((heads, 1, 1))],
        out_specs=[pl.BlockSpec((chunk, width), lambda b, c: (b * nc + c, 0)),
                   state_spec((heads, dh, dh)), state_spec((heads, 1, dh)), state_spec((heads, 1, 1))],
        out_shape=[jax.ShapeDtypeStruct((batch * seq, width), BF16),
                   jax.ShapeDtypeStruct((batch, heads, dh, dh), F32),
                   jax.ShapeDtypeStruct((batch, heads, 1, dh), F32),
                   jax.ShapeDtypeStruct((batch, heads, 1, 1), F32)],
        scratch_shapes=[pltpu.VMEM((heads, dh, dh), F32), pltpu.VMEM((heads, 1, dh), F32),
                        pltpu.VMEM((heads, 1, 1), F32)],
        compiler_params=_cparams("parallel", "arbitrary"),
        name="mlstm",
    )(qkvo, qkvo, qkvo, qkvo, gates_c, gates_r, c0, n0r, m0r)
    return hs, c1, n1.reshape(batch, heads, dh), m1.reshape(batch, heads)


def _conv_kernel(halo_ref, u_ref, dw_ref, dwb_ref, lg_ref, lb_ref, pw_ref, pwb_ref, o_ref, full_sc, sh_sc, wb_sc, y_sc,
                 *, taps, zero_first, rb, cb):
    tt, cw = u_ref.shape
    halo = halo_ref[...]
    if zero_first:
        halo = jnp.where(pl.program_id(1) == 0, 0.0, halo)
    full_sc[0:HALO, :] = halo
    full_sc[HALO:HALO + tt, :] = u_ref[...]
    span = tt + HALO - SUBLANES
    for s in range(1, SUBLANES):
        sh_sc[s - 1] = full_sc[s:s + span, :]
    for j in range(taps):
        wb_sc[j] = jnp.broadcast_to(dw_ref[j:j + 1, :], (SUBLANES, cw))
    base = HALO - (taps - 1)
    n_acc = 1

    for c0 in range(0, cw, cb):
        cols = slice(c0, c0 + cb)

        def body(r, carry):
            r0 = pl.multiple_of(r * rb, rb)
            accs = [jnp.zeros((rb, cb), F32) + dwb_ref[:, cols]] + [None] * (n_acc - 1)
            for j in range(taps):
                a, s = divmod(base + j, SUBLANES)
                rows = pl.ds(r0 + a * SUBLANES, rb)
                src = full_sc[rows, cols] if s == 0 else sh_sc[s - 1, rows, cols]
                term = jnp.concatenate([wb_sc[j, :, cols]] * (rb // SUBLANES), axis=0) * src
                accs[j % n_acc] = term if accs[j % n_acc] is None else accs[j % n_acc] + term
            y_sc[pl.ds(r0, rb), cols] = functools.reduce(lambda p, q: p + q, accs)
            return carry

        lax.fori_loop(0, tt // rb, body, 0)
    y = y_sc[...]
    mu = jnp.mean(y, axis=-1, keepdims=True)
    yc = y - mu
    yn = yc * lax.rsqrt(jnp.mean(yc * yc, axis=-1, keepdims=True) + LN_EPS) * lg_ref[...] + lb_ref[...]
    act = yn * jax.nn.sigmoid(yn)
    o_ref[...] = (jnp.dot(act.astype(BF16), pw_ref[...], preferred_element_type=F32) + pwb_ref[...]).astype(o_ref.dtype)


def _conv(u, halo_src, dw_w, dw_b, ln_g, ln_b, pw_w, pw_b, *, row0, batch, seq, tt, zero_first):
    cw = u.shape[1]
    d = pw_w.shape[1]
    taps = dw_w.shape[0]
    nt = seq // tt
    tb0 = row0 // tt
    if zero_first:
        hb0, per = row0 // HALO, tt // HALO
        halo_map = lambda b, t: (jnp.maximum(hb0 + (b * nt + t) * per - 1, 0), 0)
    else:
        assert nt == 1, "a carried conv cache is only supported for single-tile sequences"
        halo_map = lambda b, t: (b, 0)
    dw_pad = jnp.zeros((HALO, cw), F32).at[:taps].set(dw_w)
    return pl.pallas_call(
        functools.partial(_conv_kernel, taps=taps, zero_first=zero_first, rb=min(4 * SUBLANES, tt), cb=min(2 * LANES, cw)),
        grid=(batch, nt),
        in_specs=[pl.BlockSpec((HALO, cw), halo_map),
                  pl.BlockSpec((tt, cw), lambda b, t: (tb0 + b * nt + t, 0)),
                  pl.BlockSpec((HALO, cw), lambda b, t: (0, 0)),
                  pl.BlockSpec((1, cw), lambda b, t: (0, 0)),
                  pl.BlockSpec((1, cw), lambda b, t: (0, 0)),
                  pl.BlockSpec((1, cw), lambda b, t: (0, 0)),
                  pl.BlockSpec((cw, d), lambda b, t: (0, 0)),
                  pl.BlockSpec((1, d), lambda b, t: (0, 0))],
        out_specs=pl.BlockSpec((tt, d), lambda b, t: (b * nt + t, 0)),
        out_shape=jax.ShapeDtypeStruct((batch * seq, d), BF16),
        scratch_shapes=[pltpu.VMEM((HALO + tt, cw), F32),
                        pltpu.VMEM((SUBLANES - 1, HALO + tt - SUBLANES, cw), F32),
                        pltpu.VMEM((taps, SUBLANES, cw), F32),
                        pltpu.VMEM((tt, cw), F32)],
        compiler_params=_cparams("parallel", "arbitrary"),
        name="conv_module",
    )(halo_src, u, dw_pad, dw_b, ln_g, ln_b, pw_w, pw_b)


def _merge_kernel(xa_ref, xb_ref, hsa_ref, hsb_ref, boa_ref, bob_ref, gaa_ref, gab_ref, gba_ref, gbb_ref,
                  wa_ref, wo_ref, g_ref, rwh_ref, rwl_ref, rb_ref, x1_ref, xn_ref, lg_ref, *, n_exp, n_a_tiles):
    i = pl.program_id(0)
    a = jnp.dot(_rows(i, n_a_tiles, hsa_ref, hsb_ref), wa_ref[...], preferred_element_type=F32)
    bo = _rows(i, n_a_tiles, boa_ref, bob_ref)
    ga = _rows(i, n_a_tiles, gaa_ref, gab_ref)
    gb = _rows(i, n_a_tiles, gba_ref, gbb_ref)
    mix = ga.astype(F32) * a + gb.astype(F32) * bo.astype(F32)
    x1 = _rows(i, n_a_tiles, xa_ref, xb_ref) + jnp.dot(mix.astype(BF16), wo_ref[...], preferred_element_type=F32)
    x1_ref[...] = x1
    xn = _rms(x1, g_ref[...])
    _store_tile_rows(xn_ref, (), _pack_rows(xn, 1))
    hi, lo = _split_bf16(xn)
    logits = (jnp.dot(hi, rwh_ref[...], preferred_element_type=F32)
              + jnp.dot(hi, rwl_ref[...], preferred_element_type=F32)
              + jnp.dot(lo, rwh_ref[...], preferred_element_type=F32)
              + rb_ref[...])
    lane = lax.broadcasted_iota(I32, logits.shape, 1)
    lg_ref[...] = jnp.where(lane < n_exp, logits, -jnp.inf)


def _merge(xa, xb, hsa, hsb, boa, bob, gaba, gabb, w_a, w_o, g, rwh, rwl, rb, *, n_exp, tm):
    d = xa.shape[1]
    n = xa.shape[0] + xb.shape[0]
    n_a_tiles = xa.shape[0] // tm
    mw = hsa.shape[1]
    const = lambda shp: pl.BlockSpec(shp, lambda i: (0, 0), pipeline_mode=pl.Buffered(1))
    return pl.pallas_call(
        functools.partial(_merge_kernel, n_exp=n_exp, n_a_tiles=n_a_tiles),
        grid=(n // tm,),
        in_specs=_two_part(tm, d, n_a_tiles) + _two_part(tm, mw, n_a_tiles) + _two_part(tm, d, n_a_tiles)
        + _two_part(tm, d, n_a_tiles, 0) + _two_part(tm, d, n_a_tiles, 1) + [
            const((mw, d)), const((d, d)), const((1, d)),
            const((d, LANES)), const((d, LANES)), const((1, LANES))],
        out_specs=[pl.BlockSpec((tm, d), lambda i: (i, 0)),
                   pl.BlockSpec((tm * SUBLANES, LANES), lambda i: (i, 0)),
                   pl.BlockSpec((tm, LANES), lambda i: (i, 0))],
        out_shape=[jax.ShapeDtypeStruct((n, d), F32),
                   jax.ShapeDtypeStruct((n * SUBLANES, LANES), U32),
                   jax.ShapeDtypeStruct((n, LANES), F32)],
        compiler_params=_cparams("parallel"),
        name="merge",
    )(xa, xb, hsa, hsb, boa, bob, gaba, gabb, gaba, gabb, w_a, w_o, g, rwh, rwl, rb)


def _router_kernel(lg_ref, w_ref, e_ref, r_ref, cnt_ref, carry_sc):
    i = pl.program_id(0)

    @pl.when(i == 0)
    def _():
        carry_sc[...] = jnp.zeros_like(carry_sc)

    l = lg_ref[...]
    tm = l.shape[0]
    lane = lax.broadcasted_iota(I32, l.shape, 1)
    lane_f = lane.astype(F32)
    vals, hots, idxs = [], [], []
    for _ in range(TOP_K):
        mx = jnp.max(l, axis=1, keepdims=True)
        idx = jnp.min(jnp.where(l == mx, lane_f, float(LANES)), axis=1, keepdims=True)
        hot = lane_f == idx
        vals.append(mx)
        hots.append(hot)
        idxs.append(idx)
        l = jnp.where(hot, -jnp.inf, l)
    ex = [jnp.exp(v - vals[0]) for v in vals]
    tot = ex[0] + ex[1] + ex[2] + ex[3]
    sel = jnp.zeros(l.shape, F32)
    for hot in hots:
        sel = jnp.where(hot, 1.0, sel)
    rr = lax.broadcasted_iota(I32, (tm, tm), 0)
    cc = lax.broadcasted_iota(I32, (tm, tm), 1)
    tri = jnp.where(cc < rr, 1.0, 0.0).astype(BF16)
    cum = jnp.dot(tri, sel.astype(BF16), preferred_element_type=F32) + carry_sc[...]
    w_out = jnp.zeros(l.shape, F32)
    e_out = jnp.zeros(l.shape, I32)
    r_out = jnp.zeros(l.shape, I32)
    for k in range(TOP_K):
        rank = jnp.sum(jnp.where(hots[k], cum, 0.0), axis=1, keepdims=True).astype(I32)
        w_out = jnp.where(lane == k, ex[k] / tot, w_out)
        e_out = jnp.where(lane == k, idxs[k].astype(I32), e_out)
        r_out = jnp.where(lane == k, rank, r_out)
    w_ref[...] = w_out
    e_ref[...] = e_out
    r_ref[...] = r_out
    carry_sc[...] = carry_sc[...] + jnp.sum(sel, axis=0, keepdims=True)
    cnt_ref[...] = carry_sc[...]


def _router(logits, *, tm):
    n = logits.shape[0]
    blk = pl.BlockSpec((tm, LANES), lambda i: (i, 0))
    return pl.pallas_call(
        _router_kernel,
        grid=(n // tm,),
        in_specs=[blk],
        out_specs=[blk, blk, blk, pl.BlockSpec((1, LANES), lambda i: (0, 0))],
        out_shape=[jax.ShapeDtypeStruct((n, LANES), F32), jax.ShapeDtypeStruct((n, LANES), I32),
                   jax.ShapeDtypeStruct((n, LANES), I32), jax.ShapeDtypeStruct((1, LANES), F32)],
        scratch_shapes=[pltpu.VMEM((1, LANES), F32)],
        compiler_params=_cparams("arbitrary"),
        name="router",
    )(logits)


def _dispatch_kernel(pad_ref, slot_ref, x_ref, xs_ref, zero_sc, sem, zsem, *, n_exp, n_tiles):
    tm = x_ref.shape[0] // SUBLANES
    tme = zero_sc.shape[0] // SUBLANES

    def tile_rows(row, size):
        return pl.ds(pl.multiple_of(row * SUBLANES, SUBLANES), size * SUBLANES)

    def zero_fill(wait):
        def run(row, size):
            cp = pltpu.make_async_copy(zero_sc.at[pl.ds(0, size * SUBLANES)], xs_ref.at[tile_rows(row, size)], zsem)
            if wait:
                cp.wait()
            else:
                cp.start()

        def per_expert(e, carry):
            first, length = pad_ref[e], pad_ref[n_exp + e]
            size = tme // 2
            while size >= 1:
                @pl.when((length & size) != 0)
                def _(size=size):
                    run(first + (length & ~(2 * size - 1)), size)
                size //= 2
            return carry

        def per_tile(t, carry):
            run(t * tme, tme)
            return carry

        lax.fori_loop(0, n_exp, per_expert, 0)
        lax.fori_loop(pad_ref[2 * n_exp], n_tiles, per_tile, 0)

    @pl.when(pl.program_id(0) == 0)
    def _():
        zero_sc[...] = jnp.zeros_like(zero_sc)
        zero_fill(wait=False)
        zero_fill(wait=True)

    def row_copy(r, k):
        return pltpu.make_async_copy(x_ref.at[tile_rows(r, 1)], xs_ref.at[tile_rows(slot_ref[0, 0, r * TOP_K + k], 1)], sem)

    def start_rows(r, carry):
        for k in range(TOP_K):
            row_copy(r, k).start(priority=k % 2)
        return carry

    def wait_rows(r, carry):
        for k in range(TOP_K):
            row_copy(r, k).wait()
        return carry

    lax.fori_loop(0, tm, start_rows, 0, unroll=ROW_LOOP_UNROLL)
    lax.fori_loop(0, tm, wait_rows, 0, unroll=ROW_LOOP_UNROLL)


def _dispatch(pads, slots, xn, n_tiles, tme, *, n_exp, tm):
    n = xn.shape[0] // SUBLANES
    n_slots = n_tiles * tme
    return pl.pallas_call(
        functools.partial(_dispatch_kernel, n_exp=n_exp, n_tiles=n_tiles),
        grid_spec=pltpu.PrefetchScalarGridSpec(
            num_scalar_prefetch=1,
            grid=(n // tm,),
            in_specs=[pl.BlockSpec((1, 1, tm * TOP_K), lambda i, pads: (i, 0, 0), memory_space=pltpu.SMEM),
                      pl.BlockSpec((tm * SUBLANES, LANES), lambda i, pads: (i, 0))],
            out_specs=pl.BlockSpec(memory_space=pl.ANY),
            scratch_shapes=[pltpu.VMEM((tme * SUBLANES, LANES), xn.dtype), pltpu.SemaphoreType.DMA(()),
                            pltpu.SemaphoreType.DMA(())]),
        out_shape=jax.ShapeDtypeStruct((n_slots * SUBLANES, LANES), xn.dtype),
        compiler_params=_cparams("arbitrary"),
        name="dispatch",
    )(pads, slots.reshape(n // tm, 1, tm * TOP_K), xn)


def _tile_state(te_ref, axis):
    i = pl.program_id(axis)
    used = i < te_ref[pl.num_programs(axis)]
    new_weights = jnp.logical_or(i == 0, te_ref[i] != te_ref[jnp.maximum(i - 1, 0)])
    return used, new_weights


def _stream_weights(te_ref, nxt_ref, i, j, nj, used, new_weights, weights, sem):
    tn = weights[0][1].shape[1]

    def copies(e, jj):
        cols = pl.ds(jj * tn if isinstance(jj, int) else pl.multiple_of(jj * tn, tn), tn)
        return [pltpu.make_async_copy(w.at[e, :, cols], stage, sem.at[s]) for s, (w, stage, _) in enumerate(weights)]

    @pl.when(jnp.logical_and(used, new_weights))
    def _():
        @pl.when(jnp.logical_and(i == 0, j == 0))
        def _():
            for c in copies(te_ref[0], 0):
                c.start()

        for c in copies(te_ref[i], j):
            c.wait()
        for _, stage, dst in weights:
            dst[...] = stage[...].astype(BF16)
        nxt = nxt_ref[i]

        @pl.when(nxt >= 0)
        def _():
            for c in copies(nxt, j):
                c.start()

        if not (isinstance(j, int) and j + 1 >= nj):
            @pl.when(jnp.logical_and(nxt < 0, j + 1 < nj))
            def _():
                for c in copies(te_ref[0], j + 1):
                    c.start()


def _up_kernel(te_ref, nxt_ref, xs_ref, wg_ref, bg_ref, wu_ref, bu_ref, h_ref, gf_sc, uf_sc, wg_sc, wu_sc, sem):
    used, new_weights = _tile_state(te_ref, 1)
    _stream_weights(te_ref, nxt_ref, pl.program_id(1), pl.program_id(0), pl.num_programs(0), used, new_weights,
                    [(wg_ref, gf_sc, wg_sc), (wu_ref, uf_sc, wu_sc)], sem)

    @pl.when(used)
    def _():
        x = _unpack_rows(_load_tile_rows(xs_ref, (), h_ref.shape[0]), 1).astype(BF16)
        g = jnp.minimum(jnp.dot(x, wg_sc[...], preferred_element_type=F32) + bg_ref[0], SWIGLU_LIMIT)
        u = jnp.clip(jnp.dot(x, wu_sc[...], preferred_element_type=F32) + bu_ref[0], -SWIGLU_LIMIT, SWIGLU_LIMIT)
        h_ref[...] = (g * jax.nn.sigmoid(SWIGLU_ALPHA * g) * (u + 1.0)).astype(h_ref.dtype)

    @pl.when(jnp.logical_not(used))
    def _():
        h_ref[...] = jnp.zeros_like(h_ref)


def _down_kernel(te_ref, nxt_ref, h_ref, wd_ref, bd_ref, o_ref, df_sc, wd_sc, sem):
    used, new_weights = _tile_state(te_ref, 0)
    _stream_weights(te_ref, nxt_ref, pl.program_id(0), 0, 1, used, new_weights, [(wd_ref, df_sc, wd_sc)], sem)

    @pl.when(used)
    def _():
        out = jnp.dot(h_ref[...], wd_sc[...], preferred_element_type=F32) + bd_ref[0]
        _store_tile_rows(o_ref, (), _pack_rows(out, 1))

    @pl.when(jnp.logical_not(used))
    def _():
        o_ref[...] = jnp.zeros_like(o_ref)


def _used_tile(i, te):
    return jnp.where(i < te[te.shape[0] - 1], i, 0)


def _expert_up(tile_exp, next_exp, xs, wg, bg, wu, bu, *, tm, tn):
    s = xs.shape[0] // SUBLANES
    d, de = wg.shape[1], wg.shape[2]
    bspec = pl.BlockSpec((1, 1, tn), lambda j, i, te, nx: (te[i], 0, j))
    hbm = pl.BlockSpec(memory_space=pl.ANY)
    return pl.pallas_call(
        _up_kernel,
        grid_spec=pltpu.PrefetchScalarGridSpec(
            num_scalar_prefetch=2,
            grid=(de // tn, s // tm),
            in_specs=[pl.BlockSpec((tm * SUBLANES, LANES), lambda j, i, te, nx: (_used_tile(i, te), 0)),
                      hbm, bspec, hbm, bspec],
            out_specs=pl.BlockSpec((tm, tn), lambda j, i, te, nx: (i, j)),
            scratch_shapes=[pltpu.VMEM((d, tn), F32), pltpu.VMEM((d, tn), F32),
                            pltpu.VMEM((d, tn), BF16), pltpu.VMEM((d, tn), BF16),
                            pltpu.SemaphoreType.DMA((2,))]),
        out_shape=jax.ShapeDtypeStruct((s, de), BF16),
        compiler_params=_cparams("arbitrary", "arbitrary"),
        name="expert_up",
    )(tile_exp, next_exp, xs, wg, bg, wu, bu)


def _expert_down(tile_exp, next_exp, hid, wd, bd, *, tm):
    s, de = hid.shape
    d = wd.shape[2]
    return pl.pallas_call(
        _down_kernel,
        grid_spec=pltpu.PrefetchScalarGridSpec(
            num_scalar_prefetch=2,
            grid=(s // tm,),
            in_specs=[pl.BlockSpec((tm, de), lambda i, te, nx: (_used_tile(i, te), 0)),
                      pl.BlockSpec(memory_space=pl.ANY),
                      pl.BlockSpec((1, 1, d), lambda i, te, nx: (te[i], 0, 0))],
            out_specs=pl.BlockSpec((tm * SUBLANES, LANES), lambda i, te, nx: (i, 0)),
            scratch_shapes=[pltpu.VMEM((de, d), F32), pltpu.VMEM((de, d), BF16), pltpu.SemaphoreType.DMA((1,))]),
        out_shape=jax.ShapeDtypeStruct((s * SUBLANES, LANES), U32),
        compiler_params=_cparams("arbitrary"),
        name="expert_down",
    )(tile_exp, next_exp, hid, wd, bd)


def _combine_kernel(slot_ref, nslot_ref, x1_ref, tw_ref, eo_ref, p_ref, gp_ref, wpg_ref, wple_ref, gf_ref, y_ref,
                    rows_sc, sem, *, nt):
    tm = x1_ref.shape[0]
    i = pl.program_id(0)

    def tile_rows(row):
        return pl.ds(pl.multiple_of(row * SUBLANES, SUBLANES), SUBLANES)

    def row_copy(ref, buf, r, k):
        return pltpu.make_async_copy(eo_ref.at[tile_rows(ref[0, 0, r * TOP_K + k])],
                                     rows_sc.at[buf, k, tile_rows(r)], sem.at[buf])

    def start_tile(ref, buf):
        def body(r, carry):
            for k in range(TOP_K):
                row_copy(ref, buf, r, k).start(priority=k % 2)
            return carry
        lax.fori_loop(0, tm, body, 0, unroll=ROW_LOOP_UNROLL)

    @pl.when(i == 0)
    def _():
        start_tile(slot_ref, 0)

    @pl.when(i + 1 < nt)
    def _():
        start_tile(nslot_ref, (i + 1) % 2)

    buf = i % 2

    def wait_body(r, carry):
        for k in range(TOP_K):
            row_copy(slot_ref, buf, r, k).wait()
        return carry

    lax.fori_loop(0, tm, wait_body, 0, unroll=ROW_LOOP_UNROLL)
    tw = tw_ref[...]
    moe = jnp.zeros(x1_ref.shape, F32)
    for k in range(TOP_K):
        moe = moe + tw[:, k:k + 1] * _unpack_rows(_load_tile_rows(rows_sc, (buf, k), tm), 1)
    x2 = x1_ref[...] + moe
    gate = jax.nn.sigmoid(jnp.dot(_rms(x2, gp_ref[...]).astype(BF16), wpg_ref[...], preferred_element_type=F32))
    ple = jnp.dot(p_ref[...].astype(BF16), wple_ref[...], preferred_element_type=F32)
    x3 = x2 + ple * gate
    y_ref[...] = _rms(x3, gf_ref[...])


def _combine(slots, x1, topw, eo, p, g_ple, w_pg, w_ple, g_final, *, row0, tm):
    n, d = x1.shape
    n_part, pd = p.shape
    nt = n_part // tm
    t0 = row0 // tm
    const = lambda shp: pl.BlockSpec(shp, lambda i: (0, 0), pipeline_mode=pl.Buffered(1))
    slots3 = slots.reshape(n // tm, 1, tm * TOP_K)
    return pl.pallas_call(
        functools.partial(_combine_kernel, nt=nt),
        grid=(nt,),
        in_specs=[pl.BlockSpec((1, 1, tm * TOP_K), lambda i: (t0 + i, 0, 0), memory_space=pltpu.SMEM),
                  pl.BlockSpec((1, 1, tm * TOP_K), lambda i: (t0 + jnp.minimum(i + 1, nt - 1), 0, 0),
                               memory_space=pltpu.SMEM),
                  pl.BlockSpec((tm, d), lambda i: (t0 + i, 0)),
                  pl.BlockSpec((tm, LANES), lambda i: (t0 + i, 0)),
                  pl.BlockSpec(memory_space=pl.ANY),
                  pl.BlockSpec((tm, pd), lambda i: (i, 0)),
                  const((1, d)), const((d, d)), const((pd, d)), const((1, d))],
        out_specs=pl.BlockSpec((tm, d), lambda i: (i, 0)),
        out_shape=jax.ShapeDtypeStruct((n_part, d), F32),
        scratch_shapes=[pltpu.VMEM((2, TOP_K, tm * SUBLANES, LANES), eo.dtype), pltpu.SemaphoreType.DMA((2,))],
        compiler_params=_cparams("arbitrary"),
        name="combine",
    )(slots3, slots3, x1, topw, eo, p, g_ple, w_pg, w_ple, g_final)


def _pick(n, pref):
    t = pref
    while n % t:
        t //= 2
    return t


def _layer(xa, xb, pa, pb, seqs, state, params):
    (g_mix, w_in, b_in, w_a, dw_w, dw_b, ln_g, ln_b, pw_w, pw_b, w_o, g_ffn,
     r_w, r_b, wg, bg, wu, bu, wd, bd, g_ple, w_ple, w_pg, g_final) = params
    d = xa.shape[1]
    n = xa.shape[0] + xb.shape[0]
    both = math.gcd(xa.shape[0], xb.shape[0])
    heads = state[0][1].shape[1]
    dh = state[0][1].shape[2]
    mw = heads * dh
    cw = dw_w.shape[1]
    taps = dw_w.shape[0]
    n_exp = r_w.shape[1]
    de = wg.shape[2]
    row = lambda v: v.reshape(1, -1)

    o_g = 4 * mw
    o_glu = o_g + 2 * heads
    o_gate = o_glu + 2 * cw
    w_qkvo = w_in[:, :o_g].astype(BF16)
    w_gt = jnp.zeros((d, LANES), F32).at[:, :2 * heads].set(w_in[:, o_g:o_glu])
    b_gt = jnp.zeros((1, LANES), F32).at[:, :2 * heads].set(b_in[o_g:o_glu])
    w_gt_hi, w_gt_lo = _split_bf16(w_gt)
    w_ga = w_in[:, o_glu:o_glu + cw].astype(BF16)
    w_gb = w_in[:, o_glu + cw:o_gate].astype(BF16)
    w_gab = w_in[:, o_gate:].astype(BF16)

    hs, bo, gab, new_state = [], [], [], []
    pw_bf = pw_w.astype(BF16)
    for x, (batch, seq, zero_state), (c0, n0, m0, buf) in zip((xa, xb), seqs, state):
        tm = _pick(x.shape[0], 1024)
        qkvo = _norm_mm(x, row(g_mix), w_qkvo, row(b_in[:o_g]), act=None, out_dtype=BF16, tm=tm, tn=_pick(o_g, 1024))
        gab_i = _norm_mm(x, row(g_mix), w_gab, row(b_in[o_gate:]), act="sigmoid", out_dtype=BF16, tm=tm,
                         tn=_pick(2 * d, 1024))
        gates, u = _gates_glu(x, row(g_mix), w_gt_hi, w_gt_lo, b_gt, w_ga, row(b_in[o_glu:o_glu + cw]),
                              w_gb, row(b_in[o_glu + cw:o_gate]), tm=_pick(x.shape[0], 512), tn=_pick(cw, 512))
        chunk = seq
        for cand in (256, 128, 64):
            if seq % CHUNK == 0 and seq % cand == 0:
                chunk = cand
                break
        gates_r = gates[:, :2 * heads].reshape(-1, chunk, 2 * heads).transpose(0, 2, 1)
        hs_i, c1, n1, m1 = _mlstm(qkvo, gates, gates_r, c0, n0, m0, row0=0, batch=batch, seq=seq,
                                  chunk=chunk, heads=heads, dh=dh)
        if zero_state:
            halo_src = u
        else:
            halo_src = jnp.pad(buf, ((0, 0), (HALO - (taps - 1), 0), (0, 0))).reshape(batch * HALO, cw)
        bo_i = _conv(u, halo_src, dw_w, row(dw_b), row(ln_g), row(ln_b), pw_bf, row(pw_b),
                     row0=0, batch=batch, seq=seq, tt=_pick(seq, 256), zero_first=zero_state)
        hs.append(hs_i)
        bo.append(bo_i)
        gab.append(gab_i)
        u_seq = u.reshape(batch, seq, cw)
        if seq >= taps - 1:
            buf1 = u_seq[:, seq - (taps - 1):]
        else:
            buf1 = jnp.concatenate([buf, u_seq], axis=1)[:, -(taps - 1):]
        new_state.append((c1, n1, m1, buf1))

    r_pad = jnp.zeros((d, LANES), F32).at[:, :n_exp].set(r_w)
    rb_pad = jnp.zeros((1, LANES), F32).at[:, :n_exp].set(r_b)
    r_hi, r_lo = _split_bf16(r_pad)
    tmd = _pick(both, 256)
    assert d // 2 == SUBLANES * LANES, "a packed token row must fill exactly one sublane tile"
    x1, xn2, logits = _merge(xa, xb, hs[0], hs[1], bo[0], bo[1], gab[0], gab[1], w_a.astype(BF16), w_o.astype(BF16),
                             row(g_ffn), r_hi, r_lo, rb_pad, n_exp=n_exp, tm=tmd)
    topw, tope, rank, counts = _router(logits, tm=_pick(n, 512))

    tme = _pick(n * TOP_K, 512)
    cnt = counts[0, :n_exp].astype(I32)
    tiles_per = (cnt + tme - 1) // tme
    tile_end = jnp.cumsum(tiles_per)
    offs = (tile_end - tiles_per) * tme
    n_tiles = (n * TOP_K) // tme + n_exp
    tile_ids = jnp.arange(n_tiles, dtype=I32)
    tile_exp = jnp.minimum(jnp.sum(tile_end[None, :] <= tile_ids[:, None], axis=1), n_exp - 1)
    exp_ids = jnp.arange(n_exp, dtype=I32)
    later_active = jnp.logical_and(exp_ids[None, :] > exp_ids[:, None], tiles_per[None, :] > 0)
    next_active = jnp.min(jnp.where(later_active, exp_ids[None, :], n_exp), axis=1)
    next_active = jnp.where(next_active == n_exp, -1, next_active)
    next_exp = jnp.sum(jnp.where(tile_exp[:, None] == exp_ids[None, :], next_active[None, :], 0), axis=1).astype(I32)
    tile_exp = jnp.concatenate([tile_exp.astype(I32), tile_end[-1:].astype(I32)])
    pads = jnp.concatenate([offs + cnt, tiles_per * tme - cnt, tile_end[-1:]]).astype(I32)
    slots = rank[:, :TOP_K] + jnp.sum(jnp.where(tope[:, :TOP_K, None] == exp_ids, offs, 0), axis=-1)

    xs = _dispatch(pads, slots.astype(I32), xn2, n_tiles, tme, n_exp=n_exp, tm=tmd)
    hid = _expert_up(tile_exp, next_exp, xs, wg, bg[:, None, :], wu, bu[:, None, :], tm=tme, tn=_pick(de, 1024))
    eo = _expert_down(tile_exp, next_exp, hid, wd, bd[:, None, :], tm=tme)
    w_pg_bf, w_ple_bf = w_pg.astype(BF16), w_ple.astype(BF16)
    ys = [_combine(slots.astype(I32), x1, topw, eo, p, row(g_ple), w_pg_bf, w_ple_bf, row(g_final), row0=r0, tm=tmd)
          for r0, p in ((0, pa), (xa.shape[0], pb))]
    return ys, new_state


def kernel(x_prompt, x_sample, state_mlstm_C, state_mlstm_n, state_mlstm_m, cache_conv, p_prompt, p_sample, norm_mix, w_in, b_in, w_a, conv_dw_w, conv_dw_b, conv_ln_g, conv_ln_b, conv_pw_w, conv_pw_b, w_o, norm_ffn, router_w, router_b, exp_w_gate, exp_b_gate, exp_w_up, exp_b_up, exp_w_down, exp_b_down, norm_ple, w_ple, w_ple_gate, norm_final):
    depth = w_in.shape[0]
    assert depth == 1, "the final RMSNorm is fused into the single layer's last kernel"
    bp, sp, d = x_prompt.shape
    bs, ss, _ = x_sample.shape
    heads, dh = state_mlstm_C.shape[2], state_mlstm_C.shape[3]
    cw = cache_conv.shape[3]
    taps = conv_dw_w.shape[1]
    n_p, n_s = bp * sp, bs * ss
    l = 0
    zero_state = (jnp.zeros((bp, heads, dh, dh), F32), jnp.zeros((bp, heads, dh), F32),
                  jnp.zeros((bp, heads), F32), jnp.zeros((bp, taps - 1, cw), F32))
    carried = (state_mlstm_C[l].astype(F32), state_mlstm_n[l].astype(F32), state_mlstm_m[l].astype(F32), cache_conv[l])
    params = (norm_mix[l], w_in[l], b_in[l], w_a[l], conv_dw_w[l], conv_dw_b[l], conv_ln_g[l], conv_ln_b[l],
              conv_pw_w[l], conv_pw_b[l], w_o[l], norm_ffn[l], router_w[l], router_b[l],
              exp_w_gate[l], exp_b_gate[l], exp_w_up[l], exp_b_up[l], exp_w_down[l], exp_b_down[l],
              norm_ple[l], w_ple[l], w_ple_gate[l], norm_final)
    seqs = [(bp, sp, True), (bs, ss, False)]
    (y_p, y_s), ((c_p, n_pp, m_p, b_p), (c_s, n_sp, m_s, b_s)) = _layer(
        x_prompt.reshape(n_p, d), x_sample.reshape(n_s, d), p_prompt[l].reshape(n_p, -1), p_sample[l].reshape(n_s, -1),
        seqs, [zero_state, carried], params)
    st = lambda a: a[None]
    return (y_p.reshape(bp, sp, d), y_s.reshape(bs, ss, d), st(c_p), st(n_pp), st(m_p), st(b_p),
            st(c_s), st(n_sp), st(m_s), st(b_s))
```
